```python
import math
import jax, jax.numpy as jnp
from jax import lax
import numpy as np

D_MODEL = 1024
BATCH = 2
SEQ = 16384
DEPTH = 1
DEC_BATCH = 4
DEC_SEQ = 8192
PAST_LEN = 128

GRID_W = 64
N_HEADS = 8
HEAD_DIM = 64
D_ATTN = N_HEADS * HEAD_DIM
WIN_R = 8
WIN_C = 16
SSM_GROUP = 16
D_SSM = 512
N_GROUPS = D_SSM // SSM_GROUP
STATE_P = 64
DT_MIN = 1e-3
DT_MAX = 1e-1
D_IN = 3 * D_ATTN + D_SSM + 2 * D_MODEL
N_EXPERTS = 16
EC_CAPACITY = 2
D_EXPERT = 2048
EPS = 1e-6
NEG_INF = -1e9

kernel_name = "hybrid_na_s5_ec_encoder"


def rmsnorm(x, gain):
    xf = x.astype(jnp.float32)
    xf = xf * lax.rsqrt(jnp.mean(xf * xf, axis=-1, keepdims=True) + EPS)
    return xf * gain.astype(jnp.float32)


def neighbourhood_attention(q, k, v, rpb):
    b, n, h, d = q.shape
    rows = n // GRID_W
    kr = min(WIN_R, rows)
    r = jnp.arange(rows)
    r_start = jnp.clip(r - kr // 2, 0, rows - kr)
    row_idx = r_start[:, None] + jnp.arange(kr)[None, :]
    row_off = row_idx - r[:, None]
    j = jnp.arange(GRID_W)
    c_start = jnp.clip(j - WIN_C // 2, 0, GRID_W - WIN_C)
    col_ok = (j[None, :] >= c_start[:, None]) & (j[None, :] < c_start[:, None] + WIN_C)
    col_off = jnp.clip(j[None, :] - j[:, None], -(WIN_C - 1), WIN_C - 1)
    ir = (row_off + WIN_R - 1)[:, None, :, None]
    ic = (col_off + WIN_C - 1)[None, :, None, :]
    bias = rpb.astype(jnp.float32)[:, ir, ic]
    bias = jnp.where(col_ok[None, None, :, None, :], bias, NEG_INF)

    qg = q.reshape(b, rows, GRID_W, h, d)
    kg = k.reshape(b, rows, GRID_W, h, d)[:, row_idx]
    vg = v.reshape(b, rows, GRID_W, h, d)[:, row_idx]
    s = jnp.einsum('brqhd,brakhd->bhrqak', qg, kg.astype(jnp.float32)) * (d ** -0.5) + bias
    p = jax.nn.softmax(s.reshape(b, h, rows, GRID_W, kr * GRID_W), axis=-1).reshape(s.shape)
    o = jnp.einsum('bhrqak,brakhd->brqhd', p.astype(v.dtype), vg)
    return o.reshape(b, n, h * d)


def _ssm_combine(left, right):
    a1, b1 = left
    a2, b2 = right
    return a2 * a1, a2 * b1 + b2


def s5_bidirectional(u, a_re, a_im, log_dt, b_re, b_im, c_re, c_im, d_skip):
    bsz, n, _ = u.shape
    uf = u.astype(jnp.float32).reshape(bsz, n, N_GROUPS, SSM_GROUP)
    uc = uf.astype(jnp.complex64)
    y = uf * d_skip.astype(jnp.float32).reshape(N_GROUPS, SSM_GROUP)
    for direction in range(2):
        lam = lax.complex(a_re[direction].astype(jnp.float32), a_im[direction].astype(jnp.float32))
        dt = jnp.exp(log_dt[direction].astype(jnp.float32))[:, None]
        a_bar = jnp.exp(lam * dt)
        b_mat = lax.complex(b_re[direction].astype(jnp.float32), b_im[direction].astype(jnp.float32))
        b_bar = ((a_bar - 1.0) / lam)[..., None] * b_mat
        bu = jnp.einsum('blgc,gpc->blgp', uc, b_bar)
        a_seq = jnp.broadcast_to(a_bar, bu.shape)
        _, hs = lax.associative_scan(_ssm_combine, (a_seq, bu), axis=1, reverse=(direction == 1))
        c_mat = lax.complex(c_re[direction].astype(jnp.float32), c_im[direction].astype(jnp.float32))
        y = y + jnp.einsum('blgp,gcp->blgc', hs, c_mat).real
    return y.reshape(bsz, n, D_SSM)


def expert_choice_ffn(h, w_router, w_gate, w_up, w_down):
    n_tok, d = h.shape
    cap = EC_CAPACITY * n_tok // N_EXPERTS
    affinity = jax.nn.softmax(jnp.einsum('td,de->te', h.astype(jnp.float32), w_router.astype(jnp.float32)), axis=-1)
    g, idx = lax.top_k(affinity.T, cap)
    xe = h[idx]
    a = jnp.einsum('ecd,edf->ecf', xe, w_gate)
    u = jnp.einsum('ecd,edf->ecf', xe, w_up)
    ye = jnp.einsum('ecf,efd->ecd', jax.nn.silu(a) * u, w_down)
    contrib = (g[..., None] * ye.astype(jnp.float32)).reshape(-1, d)
    out = jnp.zeros((n_tok, d), jnp.float32).at[idx.reshape(-1)].add(contrib)
    return out.astype(h.dtype)


def encoder_layer(x, c, w_ada, b_ada, norm_mix, norm_ffn, w_in, q_norm, k_norm, rpb,
                  ssm_a_re, ssm_a_im, ssm_log_dt, ssm_b_re, ssm_b_im, ssm_c_re, ssm_c_im, ssm_d,
                  w_glu, w_attn_br, w_out, w_router, w_exp_gate, w_exp_up, w_exp_down):
    b, n, d = x.shape
    dt = x.dtype
    mod = jnp.einsum('bd,de->be', jax.nn.silu(c), w_ada) + b_ada
    sh1, sc1, g1, sh2, sc2, g2 = jnp.split(mod[:, None, :], 6, axis=-1)

    h = (rmsnorm(x, norm_mix) * (1.0 + sc1) + sh1).astype(dt)
    proj = jnp.einsum('bnd,de->bne', h, w_in)
    splits = [D_ATTN, 2 * D_ATTN, 3 * D_ATTN, 3 * D_ATTN + D_SSM, 3 * D_ATTN + D_SSM + D_MODEL]
    q, k, v, u, ga, gs = jnp.split(proj, splits, axis=-1)
    q = rmsnorm(q.reshape(b, n, N_HEADS, HEAD_DIM), q_norm)
    k = rmsnorm(k.reshape(b, n, N_HEADS, HEAD_DIM), k_norm)
    attn = neighbourhood_attention(q, k, v.reshape(b, n, N_HEADS, HEAD_DIM), rpb)
    attn_branch = jnp.einsum('bne,ed->bnd', attn, w_attn_br)
    ssm = s5_bidirectional(u, ssm_a_re, ssm_a_im, ssm_log_dt, ssm_b_re, ssm_b_im,
                           ssm_c_re, ssm_c_im, ssm_d)
    ssm = jax.nn.gelu(ssm).astype(dt)
    glu = jnp.einsum('bne,ed->bnd', ssm, w_glu)
    ssm_branch = glu[..., :D_MODEL] * jax.nn.sigmoid(glu[..., D_MODEL:])
    merged = jax.nn.sigmoid(ga) * attn_branch + jax.nn.sigmoid(gs) * ssm_branch
    x = x + g1 * jnp.einsum('bnd,de->bne', merged, w_out)

    h2 = (rmsnorm(x, norm_ffn) * (1.0 + sc2) + sh2).astype(dt)
    ffn = expert_choice_ffn(h2.reshape(b * n, d), w_router, w_exp_gate, w_exp_up, w_exp_down)
    x = x + g2 * ffn.reshape(b, n, d)
    return x


def setup_inputs(seed: int = 0) -> dict:
    key = jax.random.key(seed)
    ks = jax.random.split(key, 32)
    f32 = jnp.float32
    L = DEPTH

    def nrm(k, shape, scale):
        return jax.random.normal(k, shape, f32) * scale

    a_im_init = math.pi * jnp.arange(STATE_P, dtype=f32)
    return {
        "x_prompt": nrm(ks[0], (BATCH, SEQ, D_MODEL), 1.0),
        "x_sample": nrm(ks[1], (DEC_BATCH, DEC_SEQ, D_MODEL), 1.0),
        "c_prompt": nrm(ks[2], (BATCH, D_MODEL), 1.0),
        "c_sample": nrm(ks[3], (DEC_BATCH, D_MODEL), 1.0),
        "w_ada": nrm(ks[4], (L, D_MODEL, 6 * D_MODEL), 0.5 * D_MODEL ** -0.5),
        "b_ada": nrm(ks[5], (L, 6 * D_MODEL), 0.02),
        "norm_mix": 1.0 + nrm(ks[6], (L, D_MODEL), 0.02),
        "norm_ffn": 1.0 + nrm(ks[7], (L, D_MODEL), 0.02),
        "w_in": nrm(ks[8], (L, D_MODEL, D_IN), D_MODEL ** -0.5),
        "q_norm": 1.0 + nrm(ks[9], (L, HEAD_DIM), 0.02),
        "k_norm": 1.0 + nrm(ks[10], (L, HEAD_DIM), 0.02),
        "rpb": nrm(ks[11], (L, N_HEADS, 2 * WIN_R - 1, 2 * WIN_C - 1), 0.1),
        "ssm_a_re": -0.5 + nrm(ks[12], (L, 2, N_GROUPS, STATE_P), 0.02),
        "ssm_a_im": a_im_init + nrm(ks[13], (L, 2, N_GROUPS, STATE_P), 0.02),
        "ssm_log_dt": jax.random.uniform(ks[14], (L, 2, N_GROUPS), f32, math.log(DT_MIN), math.log(DT_MAX)),
        "ssm_b_re": nrm(ks[15], (L, 2, N_GROUPS, STATE_P, SSM_GROUP), (2 * SSM_GROUP) ** -0.5),
        "ssm_b_im": nrm(ks[16], (L, 2, N_GROUPS, STATE_P, SSM_GROUP), (2 * SSM_GROUP) ** -0.5),
        "ssm_c_re": nrm(ks[17], (L, 2, N_GROUPS, SSM_GROUP, STATE_P), (2 * STATE_P) ** -0.5),
        "ssm_c_im": nrm(ks[18], (L, 2, N_GROUPS, SSM_GROUP, STATE_P), (2 * STATE_P) ** -0.5),
        "ssm_d": nrm(ks[19], (L, D_SSM), 1.0),
        "w_glu": nrm(ks[20], (L, D_SSM, 2 * D_MODEL), D_SSM ** -0.5),
        "w_attn_br": nrm(ks[21], (L, D_ATTN, D_MODEL), D_ATTN ** -0.5),
        "w_out": nrm(ks[22], (L, D_MODEL, D_MODEL), D_MODEL ** -0.5),
        "w_router": nrm(ks[23], (L, D_MODEL, N_EXPERTS), D_MODEL ** -0.5),
        "w_exp_gate": nrm(ks[24], (L, N_EXPERTS, D_MODEL, D_EXPERT), D_MODEL ** -0.5),
        "w_exp_up": nrm(ks[25], (L, N_EXPERTS, D_MODEL, D_EXPERT), D_MODEL ** -0.5),
        "w_exp_down": nrm(ks[26], (L, N_EXPERTS, D_EXPERT, D_MODEL), D_EXPERT ** -0.5),
    }


def reference(x_prompt, x_sample, c_prompt, c_sample, w_ada, b_ada, norm_mix, norm_ffn, w_in,
              q_norm, k_norm, rpb, ssm_a_re, ssm_a_im, ssm_log_dt, ssm_b_re, ssm_b_im,
              ssm_c_re, ssm_c_im, ssm_d, w_glu, w_attn_br, w_out, w_router,
              w_exp_gate, w_exp_up, w_exp_down):
    params = (w_ada, b_ada, norm_mix, norm_ffn, w_in, q_norm, k_norm, rpb,
              ssm_a_re, ssm_a_im, ssm_log_dt, ssm_b_re, ssm_b_im, ssm_c_re, ssm_c_im, ssm_d,
              w_glu, w_attn_br, w_out, w_router, w_exp_gate, w_exp_up, w_exp_down)
    y_prompt = x_prompt
    y_sample = x_sample
    for layer in range(DEPTH):
        lp = [p[layer] for p in params]
        y_prompt = encoder_layer(y_prompt, c_prompt, *lp)
        y_sample = encoder_layer(y_sample, c_sample, *lp)
    return (y_prompt, y_sample)
```

```python
import functools
import math

import jax
import jax.numpy as jnp
from jax import lax
from jax.experimental import pallas as pl
from jax.experimental.pallas import tpu as pltpu

F32 = jnp.float32
BF16 = jnp.bfloat16
I32 = jnp.int32

D_MODEL = 1024
GRID_W = 64
N_HEADS = 8
HEAD_DIM = 64
D_ATTN = N_HEADS * HEAD_DIM
WIN_R = 8
WIN_C = 16
SSM_GROUP = 16
D_SSM = 512
N_GROUPS = D_SSM // SSM_GROUP
STATE_P = 64
D_IN = 3 * D_ATTN + D_SSM + 2 * D_MODEL
N_EXPERTS = 16
EC_CAPACITY = 2
D_EXPERT = 2048
EPS = 1e-6
NEG_INF = -1e9

LANES = 128
SUBLANES = 8
BF16_ROWS = 16
VMEM_LIMIT = 56 * 1024 * 1024

SSM_L = 8
SG_GROUPS = LANES // SSM_GROUP
N_SG = N_GROUPS // SG_GROUPS
SG_STATE = SG_GROUPS * STATE_P
SCAN_ROWS = SUBLANES

ATTN_ROWS = 8
SLOT_WIN = 128
CUM_W = 256


def _cparams(sem):
    return pltpu.CompilerParams(dimension_semantics=sem, vmem_limit_bytes=VMEM_LIMIT)


def _split_bf16(a):
    hi = a.astype(BF16)
    lo = (a - hi.astype(F32)).astype(BF16)
    return hi, lo


def _dot(a, b):
    return jnp.dot(a, b, preferred_element_type=F32)


def _dot3(a, b):
    ah, al = _split_bf16(a)
    bh, bl = _split_bf16(b)
    return _dot(ah, bh) + (_dot(ah, bl) + _dot(al, bh))


def _sigmoid(z):
    return 1.0 / (1.0 + jnp.exp(-z))


def _ada_kernel(c_ref, w_ref, b_ref, o_ref):
    c = c_ref[...]
    s = c * _sigmoid(c)
    o_ref[...] = _dot3(s, w_ref[...]) + b_ref[...]


def _ada(c_pad, w_ada, b_ada):
    rows = c_pad.shape[0]
    n_out = w_ada.shape[1]
    return pl.pallas_call(
        _ada_kernel,
        grid=(n_out // D_MODEL,),
        in_specs=[pl.BlockSpec((rows, D_MODEL), lambda j: (0, 0)),
                  pl.BlockSpec((D_MODEL, D_MODEL), lambda j: (0, j)),
                  pl.BlockSpec((1, D_MODEL), lambda j: (0, j))],
        out_specs=pl.BlockSpec((rows, D_MODEL), lambda j: (0, j)),
        out_shape=jax.ShapeDtypeStruct((rows, n_out), F32),
        compiler_params=_cparams(("arbitrary",)),
        name="ada",
    )(c_pad, w_ada, b_ada.reshape(1, n_out))


def _inproj_kernel(x_ref, sc_ref, sh_ref, nm_ref, w_ref, qg_ref, kg_ref, ones_ref,
                   q_ref, k_ref, v_ref, u_ref, ga_ref, gs_ref):
    x = x_ref[...]
    ms = jnp.mean(x * x, axis=-1, keepdims=True)
    xn = x * lax.rsqrt(ms + EPS) * nm_ref[...]
    h = (xn * (1.0 + sc_ref[...]) + sh_ref[...]).astype(BF16)

    def proj(lo, hi):
        return _dot(h, w_ref[:, lo:hi])

    def head_norm(z, gain):
        hi, lo = _split_bf16(z * z)
        ssum = _dot(hi, ones_ref[...]) + _dot(lo, ones_ref[...])
        return z * lax.rsqrt(ssum * (1.0 / HEAD_DIM) + EPS) * gain

    q = head_norm(proj(0, D_ATTN), qg_ref[...]) * (HEAD_DIM ** -0.5)
    q_ref[...] = q.astype(BF16)
    k = head_norm(proj(D_ATTN, 2 * D_ATTN), kg_ref[...])
    k_ref[...] = k.astype(BF16)
    v_ref[...] = proj(2 * D_ATTN, 3 * D_ATTN).astype(BF16)
    u = proj(3 * D_ATTN, 3 * D_ATTN + D_SSM)
    for sg in range(N_SG):
        u_ref[sg] = u[:, sg * LANES:(sg + 1) * LANES].astype(BF16)
    o = 3 * D_ATTN + D_SSM
    ga_ref[...] = _sigmoid(proj(o, o + D_MODEL)).astype(BF16)
    gs_ref[...] = _sigmoid(proj(o + D_MODEL, o + 2 * D_MODEL)).astype(BF16)


def _inproj(x, sc1, sh1, norm_mix, w_in_bf, qg, kg, ones_bd, bt):
    b, n, d = x.shape
    tok = lambda w: pl.BlockSpec((None, bt, w), lambda i, j: (i, j, 0))
    mod = pl.BlockSpec((None, 1, d), lambda i, j: (i, 0, 0))
    full = lambda shape: pl.BlockSpec(shape, lambda i, j: tuple(0 for _ in shape))
    return pl.pallas_call(
        _inproj_kernel,
        grid=(b, n // bt),
        in_specs=[tok(d), mod, mod, full((1, d)), full((d, D_IN)),
                  full((1, D_ATTN)), full((1, D_ATTN)), full((D_ATTN, D_ATTN))],
        out_specs=[tok(D_ATTN), tok(D_ATTN), tok(D_ATTN),
                   pl.BlockSpec((None, N_SG, bt, LANES), lambda i, j: (i, 0, j, 0)),
                   tok(d), tok(d)],
        out_shape=[jax.ShapeDtypeStruct((b, n, D_ATTN), BF16)] * 3
        + [jax.ShapeDtypeStruct((b, N_SG, n, LANES), BF16)]
        + [jax.ShapeDtypeStruct((b, n, d), BF16)] * 2,
        compiler_params=_cparams(("parallel", "parallel")),
        name="inproj",
    )(x, sc1, sh1, norm_mix, w_in_bf, qg, kg, ones_bd)


def _attn_kernel(q_ref, kp_ref, kc_ref, kn_ref, vp_ref, vc_ref, vn_ref, bias_ref, o_ref,
                 kwin, vwin, *, rows):
    r0 = pl.program_id(1) * ATTN_ROWS
    rw = ATTN_ROWS * GRID_W
    for t, (kr, vr) in enumerate(((kp_ref, vp_ref), (kc_ref, vc_ref), (kn_ref, vn_ref))):
        kwin[t * rw:(t + 1) * rw, :] = kr[...]
        vwin[t * rw:(t + 1) * rw, :] = vr[...]
    even = lax.broadcasted_iota(I32, (GRID_W, LANES), 1) < HEAD_DIM
    nkeys = WIN_R * GRID_W

    def row_body(i, carry):
        r = r0 + i
        rs = jnp.clip(r - WIN_R // 2, 0, rows - WIN_R)
        variant = r - rs
        koff = pl.multiple_of((rs - r0 + ATTN_ROWS) * GRID_W, GRID_W)
        qoff = pl.multiple_of(i * GRID_W, GRID_W)
        for hp in range(N_HEADS // 2):
            ls = slice(hp * LANES, (hp + 1) * LANES)
            qp = q_ref[pl.ds(qoff, GRID_W), ls]
            kp = kwin[pl.ds(koff, nkeys), ls]
            vp = vwin[pl.ds(koff, nkeys), ls]
            zero = jnp.zeros_like(qp)
            outs = []
            for par in range(2):
                qh = jnp.where(even, qp, zero) if par == 0 else jnp.where(even, zero, qp)
                s = lax.dot_general(qh, kp, (((1,), (1,)), ((), ())), preferred_element_type=F32)
                s = s + bias_ref[variant, 2 * hp + par]
                m = jnp.max(s, axis=-1, keepdims=True)
                p = jnp.exp(s - m)
                l = jnp.sum(p, axis=-1, keepdims=True)
                outs.append(_dot(p.astype(BF16), vp) / l)
            o_ref[pl.ds(qoff, GRID_W), ls] = jnp.where(even, outs[0], outs[1]).astype(BF16)
        return carry

    lax.fori_loop(0, ATTN_ROWS, row_body, 0)


def _attention(q, k, v, bias_tab):
    b, n, _ = q.shape
    rows = n // GRID_W
    assert rows % ATTN_ROWS == 0 and rows >= WIN_R
    nblk = rows // ATTN_ROWS
    rw = ATTN_ROWS * GRID_W
    cur = pl.BlockSpec((None, rw, D_ATTN), lambda i, j: (i, j, 0))
    prv = pl.BlockSpec((None, rw, D_ATTN), lambda i, j: (i, jnp.maximum(j - 1, 0), 0))
    nxt = pl.BlockSpec((None, rw, D_ATTN), lambda i, j: (i, jnp.minimum(j + 1, nblk - 1), 0))
    return pl.pallas_call(
        functools.partial(_attn_kernel, rows=rows),
        grid=(b, nblk),
        in_specs=[cur, prv, cur, nxt, prv, cur, nxt,
                  pl.BlockSpec(bias_tab.shape, lambda i, j: (0, 0, 0, 0))],
        out_specs=cur,
        out_shape=jax.ShapeDtypeStruct((b, n, D_ATTN), BF16),
        scratch_shapes=[pltpu.VMEM((3 * rw, D_ATTN), BF16), pltpu.VMEM((3 * rw, D_ATTN), BF16)],
        compiler_params=_cparams(("parallel", "parallel")),
        name="attn",
    )(q, k, k, k, v, v, v, bias_tab)


def _attn_bias_table(rpb):
    var = jnp.arange(WIN_R)
    a = jnp.arange(WIN_R)
    j = jnp.arange(GRID_W)
    c_start = jnp.clip(j - WIN_C // 2, 0, GRID_W - WIN_C)
    col_ok = (j[None, :] >= c_start[:, None]) & (j[None, :] < c_start[:, None] + WIN_C)
    col_off = jnp.clip(j[None, :] - j[:, None], -(WIN_C - 1), WIN_C - 1) + WIN_C - 1
    row_off = a[None, :] - var[:, None] + WIN_R - 1
    tab = rpb.astype(F32)[:, row_off[:, None, :, None], col_off[None, :, None, :]]
    tab = jnp.where(col_ok[None, None, :, None, :], tab, NEG_INF)
    return tab.transpose(1, 0, 2, 3, 4).reshape(WIN_R, N_HEADS, GRID_W, WIN_R * GRID_W)


def _ssm_tables(a_re, a_im, log_dt, b_re, b_im, c_re, c_im, d_skip):
    L = SSM_L
    hp = lax.Precision.HIGHEST
    lam_re, lam_im = a_re.astype(F32), a_im.astype(F32)
    dt = jnp.exp(log_dt.astype(F32))[..., None]
    ldt_re, ldt_im = lam_re * dt, lam_im * dt

    def apow(kk):
        mag = jnp.exp(ldt_re * kk)
        return mag * jnp.cos(ldt_im * kk), mag * jnp.sin(ldt_im * kk)

    lam = lax.complex(lam_re, lam_im)
    a1_re, a1_im = apow(1.0)
    coef = (lax.complex(a1_re, a1_im) - 1.0) / lam
    bbar = coef[..., None] * lax.complex(b_re.astype(F32), b_im.astype(F32))
    bb_re, bb_im = jnp.real(bbar), jnp.imag(bbar)
    cm_re, cm_im = c_re.astype(F32), c_im.astype(F32)

    lags = jnp.arange(L, dtype=F32)
    pw_re, pw_im = apow(lags[:, None, None, None])
    e_re = cm_re[None] * pw_re[:, :, :, None, :] - cm_im[None] * pw_im[:, :, :, None, :]
    e_im = cm_re[None] * pw_im[:, :, :, None, :] + cm_im[None] * pw_re[:, :, :, None, :]
    kern = (jnp.einsum('ldgxp,dgpc->ldgxc', e_re, bb_re, precision=hp)
            - jnp.einsum('ldgxp,dgpc->ldgxc', e_im, bb_im, precision=hp))
    s_idx = jnp.arange(L)[:, None]
    t_idx = jnp.arange(L)[None, :]
    lag_f = jnp.clip(t_idx - s_idx, 0, L - 1)
    lag_b = jnp.clip(s_idx - t_idx, 0, L - 1)
    mf = jnp.where((t_idx >= s_idx)[:, :, None, None, None], kern[lag_f, 0], 0.0)
    mb = jnp.where((s_idx >= t_idx)[:, :, None, None, None], kern[lag_b, 1], 0.0)
    m = (mf + mb).transpose(2, 0, 4, 1, 3)
    eye = jnp.eye(SG_GROUPS, dtype=F32)
    m = m.reshape(N_SG, SG_GROUPS, L, SSM_GROUP, L, SSM_GROUP)
    t_mat = jnp.einsum('qgsctd,gh->qsgcthd', m, eye).reshape(N_SG, L * LANES, L * LANES)

    tau = jnp.arange(L, dtype=F32)[:, None, None]
    parts = []
    for d, expo in ((0, L - 1 - tau), (1, tau)):
        p_re = jnp.exp(ldt_re[d][None] * expo) * jnp.cos(ldt_im[d][None] * expo)
        p_im = jnp.exp(ldt_re[d][None] * expo) * jnp.sin(ldt_im[d][None] * expo)
        br, bi = bb_re[d].transpose(0, 2, 1), bb_im[d].transpose(0, 2, 1)
        parts.append(p_re[:, :, None, :] * br[None] - p_im[:, :, None, :] * bi[None])
        parts.append(p_re[:, :, None, :] * bi[None] + p_im[:, :, None, :] * br[None])
    wb = jnp.stack(parts, axis=0)
    wb = wb.reshape(4, L, N_SG, SG_GROUPS, SSM_GROUP, STATE_P)
    wb_mat = jnp.einsum('mtqgcp,gh->qtgcmhp', wb, eye).reshape(N_SG, L * LANES, 4 * SG_STATE)

    parts = []
    for d, expo in ((0, tau + 1.0), (1, L - tau)):
        p_re = jnp.exp(ldt_re[d][None] * expo) * jnp.cos(ldt_im[d][None] * expo)
        p_im = jnp.exp(ldt_re[d][None] * expo) * jnp.sin(ldt_im[d][None] * expo)
        cr, ci = cm_re[d], cm_im[d]
        z_re = cr[None] * p_re[:, :, None, :] - ci[None] * p_im[:, :, None, :]
        z_im = cr[None] * p_im[:, :, None, :] + ci[None] * p_re[:, :, None, :]
        parts.append(z_re.transpose(1, 3, 0, 2))
        parts.append(-z_im.transpose(1, 3, 0, 2))
    wc = jnp.stack(parts, axis=0).reshape(4, N_SG, SG_GROUPS, STATE_P, L, SSM_GROUP)
    wc_mat = jnp.einsum('mqgptc,gh->qmgpthc', wc, eye).reshape(N_SG, 4 * SG_STATE, L * LANES)

    row = jnp.arange(SCAN_ROWS)
    tiles = []
    for d in range(2):
        kinds = []
        for sh in (1, 2, 4):
            keep = (row >= sh) if d == 0 else (row <= SCAN_ROWS - 1 - sh)
            kinds.append((jnp.full((SCAN_ROWS,), float(L * sh), F32), keep))
        expo = (row + 1.0) if d == 0 else (SCAN_ROWS - row).astype(F32)
        kinds.append((L * expo.astype(F32), jnp.ones((SCAN_ROWS,), bool)))
        per_kind = []
        for expo_r, keep in kinds:
            e = expo_r[:, None, None]
            p_re = jnp.exp(ldt_re[d][None] * e) * jnp.cos(ldt_im[d][None] * e)
            p_im = jnp.exp(ldt_re[d][None] * e) * jnp.sin(ldt_im[d][None] * e)
            k3 = keep[:, None, None]
            per_kind.append(jnp.stack([jnp.where(k3, p_re, 0.0), jnp.where(k3, p_im, 0.0)], axis=0))
        tiles.append(jnp.stack(per_kind, axis=0))
    cst = jnp.stack(tiles, axis=0).reshape(2, 4, 2, SCAN_ROWS, N_SG, SG_STATE)
    cst = cst.transpose(4, 0, 1, 2, 3, 5)

    dsk = jnp.tile(d_skip.astype(F32).reshape(N_SG, 1, LANES), (1, 1, L))
    return t_mat.astype(BF16), wb_mat.astype(BF16), wc_mat.astype(BF16), cst, dsk


def _ssm_kernel(u_ref, t_ref, wb_ref, wc_ref, cst_ref, d_ref, y_ref, st_ref, *, n_chunks, mm_rows):
    n_mm = n_chunks // mm_rows
    n_tiles = n_chunks // SCAN_ROWS
    half = SG_STATE
    row_id = lax.broadcasted_iota(I32, (SCAN_ROWS, half), 0)

    def intra(c, carry):
        rows = pl.ds(pl.multiple_of(c * mm_rows, mm_rows), mm_rows)
        u = u_ref[rows, :]
        y_ref[rows, :] = _dot(u, t_ref[...]) + u.astype(F32) * d_ref[...]
        return carry

    lax.fori_loop(0, n_mm, intra, 0)

    for d in range(2):
        def inject(c, carry, d=d):
            rows = pl.ds(pl.multiple_of(c * mm_rows, mm_rows), mm_rows)
            st_ref[rows, :] = _dot(u_ref[rows, :], wb_ref[:, 2 * d * half:2 * (d + 1) * half])
            return carry

        lax.fori_loop(0, n_mm, inject, 0)

        def scan_tile(i, carry, d=d):
            k = i if d == 0 else n_tiles - 1 - i
            rows = pl.ds(pl.multiple_of(k * SCAN_ROWS, SCAN_ROWS), SCAN_ROWS)
            xr = st_ref[rows, 0:half]
            xi = st_ref[rows, half:2 * half]
            for si, sh in enumerate((1, 2, 4)):
                ar, ai = cst_ref[d, si, 0], cst_ref[d, si, 1]
                shift = sh if d == 0 else SCAN_ROWS - sh
                pr, pi = pltpu.roll(xr, shift, 0), pltpu.roll(xi, shift, 0)
                xr, xi = xr + (ar * pr - ai * pi), xi + (ar * pi + ai * pr)
            cr, ci = carry
            ar, ai = cst_ref[d, 3, 0], cst_ref[d, 3, 1]
            xr, xi = xr + (ar * cr - ai * ci), xi + (ar * ci + ai * cr)
            if d == 0:
                edge, shift, last = 0, 1, SCAN_ROWS - 1
            else:
                edge, shift, last = SCAN_ROWS - 1, SCAN_ROWS - 1, 0
            st_ref[rows, 0:half] = jnp.where(row_id == edge, cr, pltpu.roll(xr, shift, 0))
            st_ref[rows, half:2 * half] = jnp.where(row_id == edge, ci, pltpu.roll(xi, shift, 0))
            return xr[last:last + 1, :], xi[last:last + 1, :]

        zero = jnp.zeros((1, half), F32)
        lax.fori_loop(0, n_tiles, scan_tile, (zero, zero))

        def eject(c, carry, d=d):
            rows = pl.ds(pl.multiple_of(c * mm_rows, mm_rows), mm_rows)
            y_ref[rows, :] += _dot(st_ref[rows, :].astype(BF16), wc_ref[2 * d * half:2 * (d + 1) * half, :])
            return carry

        lax.fori_loop(0, n_mm, eject, 0)


def _ssm(u4, tabs):
    t_mat, wb_mat, wc_mat, cst, dsk = tabs
    b, _, n, _ = u4.shape
    lw = SSM_L * LANES
    n_chunks = n // SSM_L
    mm_rows = min(256, n_chunks)
    assert n_chunks % mm_rows == 0 and n_chunks % SCAN_ROWS == 0
    u = u4.reshape(b, N_SG, n_chunks, lw)
    one = pl.Buffered(1)
    wspec = lambda shape: pl.BlockSpec((None,) + shape, lambda q, i: (q,) + tuple(0 for _ in shape),
                                       pipeline_mode=one)
    y = pl.pallas_call(
        functools.partial(_ssm_kernel, n_chunks=n_chunks, mm_rows=mm_rows),
        grid=(N_SG, b),
        in_specs=[pl.BlockSpec((None, None, n_chunks, lw), lambda q, i: (i, q, 0, 0)),
                  wspec((lw, lw)), wspec((lw, 4 * SG_STATE)), wspec((4 * SG_STATE, lw)),
                  wspec((2, 4, 2, SCAN_ROWS, SG_STATE)), wspec((1, lw))],
        out_specs=pl.BlockSpec((None, None, n_chunks, lw), lambda q, i: (i, q, 0, 0)),
        out_shape=jax.ShapeDtypeStruct((b, N_SG, n_chunks, lw), F32),
        scratch_shapes=[pltpu.VMEM((n_chunks, 2 * SG_STATE), F32)],
        compiler_params=_cparams(("arbitrary", "arbitrary")),
        name="ssm",
    )(u, t_mat, wb_mat, wc_mat, cst, dsk)
    return y.reshape(b, N_SG, n, LANES)


def _merge_kernel(x_ref, attn_ref, y_ref, ga_ref, gs_ref, g1_ref, sc_ref, sh_ref, nf_ref,
                  wab_ref, wglu_ref, wout_ref, wr_ref, x1_ref, h2_ref, lg_ref):
    ab = _dot(attn_ref[...], wab_ref[...])
    y = jnp.concatenate([y_ref[sg] for sg in range(N_SG)], axis=-1)
    gel = 0.5 * y * (1.0 + jnp.tanh(math.sqrt(2.0 / math.pi) * (y + 0.044715 * (y * y * y))))
    glu = _dot(gel.astype(BF16), wglu_ref[...])
    sb = glu[:, :D_MODEL] * _sigmoid(glu[:, D_MODEL:])
    merged = ga_ref[...].astype(F32) * ab + gs_ref[...].astype(F32) * sb
    x1 = x_ref[...] + g1_ref[...] * _dot(merged.astype(BF16), wout_ref[...])
    x1_ref[...] = x1
    ms = jnp.mean(x1 * x1, axis=-1, keepdims=True)
    h2 = x1 * lax.rsqrt(ms + EPS) * nf_ref[...] * (1.0 + sc_ref[...]) + sh_ref[...]
    h2_ref[...] = h2.astype(BF16)
    lg_ref[...] = _dot3(h2, wr_ref[...])


def _merge(x, attn, y4, ga, gs, g1, sc2, sh2, norm_ffn, wab, wglu, wout, wr_pad, bt):
    b, n, d = x.shape
    tok = lambda w: pl.BlockSpec((None, bt, w), lambda i, j: (i, j, 0))
    mod = pl.BlockSpec((None, 1, d), lambda i, j: (i, 0, 0))
    full = lambda shape: pl.BlockSpec(shape, lambda i, j: tuple(0 for _ in shape))
    return pl.pallas_call(
        _merge_kernel,
        grid=(b, n // bt),
        in_specs=[tok(d), tok(D_ATTN),
                  pl.BlockSpec((None, N_SG, bt, LANES), lambda i, j: (i, 0, j, 0)),
                  tok(d), tok(d), mod, mod, mod, full((1, d)),
                  full((D_ATTN, d)), full((D_SSM, 2 * d)), full((d, d)), full((d, LANES))],
        out_specs=[tok(d), tok(d), tok(LANES)],
        out_shape=[jax.ShapeDtypeStruct((b, n, d), F32), jax.ShapeDtypeStruct((b, n, d), BF16),
                   jax.ShapeDtypeStruct((b, n, LANES), F32)],
        compiler_params=_cparams(("parallel", "parallel")),
        name="merge",
    )(x, attn, y4, ga, gs, g1, sc2, sh2, norm_ffn, wab, wglu, wout, wr_pad)


def _route_kernel(lg_ref, tri_ref, aff_ref, slot_ref, *, cap):
    lg = lg_ref[...]
    m = jnp.max(lg, axis=1, keepdims=True)
    e = jnp.exp(lg - m)
    aff = e / jnp.sum(e, axis=1, keepdims=True)
    aff_ref[...] = aff
    bits = lax.bitcast_convert_type(aff, I32)

    def count(mask):
        c = jnp.sum(jnp.where(mask, 1.0, 0.0), axis=0, keepdims=True)
        return jnp.sum(c, axis=2, keepdims=True)

    def bit_step(i, thr):
        cand = thr | (jnp.int32(1) << (30 - i))
        return jnp.where(count(bits >= cand) >= cap, cand, thr)

    thr3 = lax.fori_loop(0, 31, bit_step, jnp.zeros((1, N_EXPERTS, 1), I32))
    need = (cap - count(bits > thr3))[0]
    thr = thr3[0]
    n_chunks = lg.shape[0]
    carry0 = jnp.zeros((N_EXPERTS, 1), F32)

    def prefix(flag, carry):
        f = jnp.where(flag, 1.0, 0.0)
        inc = _dot(f.astype(BF16), tri_ref[...]) + carry
        return inc, inc - f

    def tie_body(c, carry):
        b = lax.bitcast_convert_type(aff_ref[c], I32)
        tie = b == thr
        inc, rank = prefix(tie, carry)
        sel = (b > thr) | (tie & (rank < need))
        slot_ref[c] = jnp.where(sel, 1, 0).astype(I32)
        return inc[:, CUM_W - 1:CUM_W]

    lax.fori_loop(0, n_chunks, tie_body, carry0)

    def slot_body(c, carry):
        sel = slot_ref[c] > 0
        inc, excl = prefix(sel, carry)
        slot_ref[c] = jnp.where(sel, excl, -1.0).astype(I32)
        return inc[:, CUM_W - 1:CUM_W]

    lax.fori_loop(0, n_chunks, slot_body, carry0)


def _route(logits_c, tri, cap):
    nc = logits_c.shape[0]
    shp = (nc, N_EXPERTS, CUM_W)
    return pl.pallas_call(
        functools.partial(_route_kernel, cap=cap),
        grid=(1,),
        in_specs=[pl.BlockSpec(shp, lambda i: (0, 0, 0)), pl.BlockSpec((CUM_W, CUM_W), lambda i: (0, 0))],
        out_specs=[pl.BlockSpec(shp, lambda i: (0, 0, 0))] * 2,
        out_shape=[jax.ShapeDtypeStruct(shp, F32), jax.ShapeDtypeStruct(shp, I32)],
        compiler_params=_cparams(("arbitrary",)),
        name="route",
    )(logits_c, tri)


def _gather_kernel(offs_ref, h_ref, slot_ref, xe_ref, *, nb, cap, win):
    e = pl.program_id(0)
    blk = pl.program_id(1)

    @pl.when(blk == 0)
    def _():
        xe_ref[...] = jnp.zeros_like(xe_ref)

    off = offs_ref[e * (nb + 1) + blk]
    end = offs_ref[e * (nb + 1) + blk + 1]
    a0 = (off // BF16_ROWS) * BF16_ROWS
    n_win = jnp.where(end > off, (end - a0 + win - 1) // win, 0)
    slot = slot_ref[...]
    riota = lax.broadcasted_iota(I32, (win, 1), 0)

    def body(w, carry):
        nominal = a0 + w * win
        start = pl.multiple_of(jnp.minimum(nominal, cap - win), BF16_ROWS)
        hit = (slot == riota + start) & (slot >= nominal)
        onehot = jnp.where(hit, 1.0, 0.0).astype(BF16)
        picked = _dot(onehot, h_ref[...]).astype(BF16)
        xe_ref[pl.ds(start, win), :] += picked
        return carry

    lax.fori_loop(0, n_win, body, 0)


def _gather(offs_flat, h2, slot3, cap, bt):
    t, d = h2.shape
    nb = t // bt
    win = min(SLOT_WIN, cap)
    gs = pltpu.PrefetchScalarGridSpec(
        num_scalar_prefetch=1,
        grid=(N_EXPERTS, nb),
        in_specs=[pl.BlockSpec((bt, d), lambda e, j, offs: (j, 0)),
                  pl.BlockSpec((None, 1, bt), lambda e, j, offs: (e, 0, j))],
        out_specs=pl.BlockSpec((None, cap, d), lambda e, j, offs: (e, 0, 0)),
    )
    return pl.pallas_call(
        functools.partial(_gather_kernel, nb=nb, cap=cap, win=win),
        grid_spec=gs,
        out_shape=jax.ShapeDtypeStruct((N_EXPERTS, cap, d), BF16),
        compiler_params=_cparams(("arbitrary", "arbitrary")),
        name="gather",
    )(offs_flat, h2, slot3)


def _ffn_kernel(x_ref, wg_ref, wu_ref, wd_ref, y_ref, *, fchunk):
    x = x_ref[...]
    acc = jnp.zeros((x.shape[0], D_MODEL), F32)
    for f in range(D_EXPERT // fchunk):
        fs = slice(f * fchunk, (f + 1) * fchunk)
        a = _dot(x, wg_ref[:, fs])
        u = _dot(x, wu_ref[:, fs])
        hmid = (a * _sigmoid(a) * u).astype(BF16)
        acc = acc + _dot(hmid, wd_ref[fs, :])
    y_ref[...] = acc.astype(BF16)


def _ffn(xe, wg, wu, wd, tm):
    e, cap, d = xe.shape
    return pl.pallas_call(
        functools.partial(_ffn_kernel, fchunk=512),
        grid=(e, cap // tm),
        in_specs=[pl.BlockSpec((None, tm, d), lambda i, j: (i, j, 0)),
                  pl.BlockSpec((None, d, D_EXPERT), lambda i, j: (i, 0, 0)),
                  pl.BlockSpec((None, d, D_EXPERT), lambda i, j: (i, 0, 0)),
                  pl.BlockSpec((None, D_EXPERT, d), lambda i, j: (i, 0, 0))],
        out_specs=pl.BlockSpec((None, tm, d), lambda i, j: (i, j, 0)),
        out_shape=jax.ShapeDtypeStruct((e, cap, d), BF16),
        compiler_params=_cparams(("parallel", "parallel")),
        name="ffn",
    )(xe, wg, wu, wd)


def _window_copy(ye_hbm, buf, sem, e, start, win):
    return pltpu.make_async_copy(ye_hbm.at[e, pl.ds(start, win), :], buf, sem)


def _combine_kernel(offs_ref, x1_ref, slot_ref, g_ref, g2_ref, ye_hbm, o_ref, ybuf, xbuf, acc_ref, sems, xsem,
                    *, nb, cap, win):
    blk = pl.program_id(0)
    starts = []
    for e in range(N_EXPERTS):
        off = offs_ref[e * (nb + 1) + blk]
        a0 = (off // BF16_ROWS) * BF16_ROWS
        start = pl.multiple_of(jnp.minimum(a0, cap - win), BF16_ROWS)
        starts.append((a0, start))
        _window_copy(ye_hbm, ybuf.at[e], sems.at[e], e, start, win).start()

    acc_ref[...] = jnp.zeros_like(acc_ref)
    liota = lax.broadcasted_iota(I32, (1, win), 1)
    slots = slot_ref[...]
    gates = g_ref[...]
    for e in range(N_EXPERTS):
        a0, start = starts[e]
        scol = slots[:, e:e + 1]
        gcol = gates[:, e:e + 1]
        _window_copy(ye_hbm, ybuf.at[e], sems.at[e], e, start, win).wait()
        onehot = jnp.where(scol == liota + start, 1.0, 0.0).astype(BF16)
        acc_ref[...] += gcol * _dot(onehot, ybuf[e])

        end = offs_ref[e * (nb + 1) + blk + 1]
        n_win = (end - a0 + win - 1) // win

        def extra(w, carry, e=e, a0=a0, scol=scol, gcol=gcol):
            nominal = a0 + w * win
            st = pl.multiple_of(jnp.minimum(nominal, cap - win), BF16_ROWS)
            cp = _window_copy(ye_hbm, xbuf, xsem, e, st, win)
            cp.start()
            cp.wait()
            hit = (scol == liota + st) & (scol >= nominal)
            acc_ref[...] += gcol * _dot(jnp.where(hit, 1.0, 0.0).astype(BF16), xbuf[...])
            return carry

        lax.fori_loop(1, n_win, extra, 0)

    o_ref[...] = x1_ref[...] + g2_ref[...] * acc_ref[...]


def _combine(offs_flat, x1, slot_t, gate_t, g2, ye, n_per_batch, bt):
    t, d = x1.shape
    nb = t // bt
    cap = ye.shape[1]
    win = min(SLOT_WIN, cap)
    per = n_per_batch // bt
    gs = pltpu.PrefetchScalarGridSpec(
        num_scalar_prefetch=1,
        grid=(nb,),
        in_specs=[pl.BlockSpec((bt, d), lambda j, offs: (j, 0)),
                  pl.BlockSpec((bt, N_EXPERTS), lambda j, offs: (j, 0)),
                  pl.BlockSpec((bt, N_EXPERTS), lambda j, offs: (j, 0)),
                  pl.BlockSpec((None, 1, d), lambda j, offs: (j // per, 0, 0)),
                  pl.BlockSpec(memory_space=pl.ANY)],
        out_specs=pl.BlockSpec((bt, d), lambda j, offs: (j, 0)),
        scratch_shapes=[pltpu.VMEM((N_EXPERTS, win, d), BF16), pltpu.VMEM((win, d), BF16),
                        pltpu.VMEM((bt, d), F32),
                        pltpu.SemaphoreType.DMA((N_EXPERTS,)), pltpu.SemaphoreType.DMA(())],
    )
    return pl.pallas_call(
        functools.partial(_combine_kernel, nb=nb, cap=cap, win=win),
        grid_spec=gs,
        out_shape=jax.ShapeDtypeStruct((t, d), F32),
        compiler_params=_cparams(("arbitrary",)),
        name="combine",
    )(offs_flat, x1, slot_t, gate_t, g2, ye)


def _prep_weights(w_in, q_norm, k_norm, rpb, ssm_params, w_glu, w_attn_br, w_out, w_router,
                  w_exp_gate, w_exp_up, w_exp_down):
    head = jnp.arange(D_ATTN) // HEAD_DIM
    return dict(
        w_in=w_in.astype(BF16),
        qg=jnp.tile(q_norm.astype(F32), N_HEADS).reshape(1, D_ATTN),
        kg=jnp.tile(k_norm.astype(F32), N_HEADS).reshape(1, D_ATTN),
        ones_bd=(head[:, None] == head[None, :]).astype(BF16),
        bias_tab=_attn_bias_table(rpb),
        ssm=_ssm_tables(*ssm_params),
        wglu=w_glu.astype(BF16), wab=w_attn_br.astype(BF16), wout=w_out.astype(BF16),
        wr=jnp.pad(w_router.astype(F32), ((0, 0), (0, LANES - N_EXPERTS))),
        wg=w_exp_gate.astype(BF16), wu=w_exp_up.astype(BF16), wd=w_exp_down.astype(BF16),
        tri=(jnp.arange(CUM_W)[:, None] <= jnp.arange(CUM_W)[None, :]).astype(BF16),
    )


def _token_block(n, want):
    bt = min(want, n)
    assert n % bt == 0
    return bt


def _encoder_layer(x, c, w_ada, b_ada, norm_mix, norm_ffn, wts):
    b, n, d = x.shape
    t = b * n
    cap = EC_CAPACITY * t // N_EXPERTS

    c_pad = jnp.pad(c.astype(F32), ((0, (-b) % SUBLANES), (0, 0)))
    mod = _ada(c_pad, w_ada, b_ada)[:b]
    sh1, sc1, g1, sh2, sc2, g2 = [m.reshape(b, 1, d) for m in jnp.split(mod, 6, axis=-1)]

    bt = _token_block(n, 512)
    q, k, v, u4, ga, gs = _inproj(x, sc1, sh1, norm_mix.reshape(1, d), wts["w_in"], wts["qg"], wts["kg"],
                                  wts["ones_bd"], bt)
    attn = _attention(q, k, v, wts["bias_tab"])
    y4 = _ssm(u4, wts["ssm"])
    x1, h2, logits = _merge(x, attn, y4, ga, gs, g1, sc2, sh2, norm_ffn.reshape(1, d),
                            wts["wab"], wts["wglu"], wts["wout"], wts["wr"], bt)

    lg = logits.reshape(t, LANES)[:, :N_EXPERTS]
    lg_c = lg.reshape(t // CUM_W, CUM_W, N_EXPERTS).transpose(0, 2, 1)
    aff_c, slot_c = _route(lg_c, wts["tri"], cap)
    slot_et = slot_c.transpose(1, 0, 2).reshape(N_EXPERTS, t)
    slot_te = slot_et.T
    gate_te = aff_c.transpose(0, 2, 1).reshape(t, N_EXPERTS)

    bt2 = _token_block(t, 512)
    nb = t // bt2
    sel_cnt = (slot_et >= 0).astype(I32).reshape(N_EXPERTS, nb, bt2).sum(axis=-1)
    offs = jnp.concatenate([jnp.zeros((N_EXPERTS, 1), I32), jnp.cumsum(sel_cnt, axis=-1)], axis=-1)
    offs_flat = offs.reshape(-1).astype(I32)

    h2f = h2.reshape(t, d)
    xe = _gather(offs_flat, h2f, slot_et.reshape(N_EXPERTS, 1, t), cap, bt2)
    ye = _ffn(xe, wts["wg"], wts["wu"], wts["wd"], _token_block(cap, 1024))
    out = _combine(offs_flat, x1.reshape(t, d), slot_te, gate_te, g2, ye, n, bt2)
    return out.reshape(b, n, d)


def kernel(x_prompt, x_sample, c_prompt, c_sample, w_ada, b_ada, norm_mix, norm_ffn, w_in, q_norm, k_norm, rpb,
           ssm_a_re, ssm_a_im, ssm_log_dt, ssm_b_re, ssm_b_im, ssm_c_re, ssm_c_im, ssm_d, w_glu, w_attn_br,
           w_out, w_router, w_exp_gate, w_exp_up, w_exp_down):
    y_prompt, y_sample = x_prompt, x_sample
    for layer in range(w_ada.shape[0]):
        ssm_params = tuple(p[layer] for p in (ssm_a_re, ssm_a_im, ssm_log_dt, ssm_b_re, ssm_b_im,
                                              ssm_c_re, ssm_c_im, ssm_d))
        wts = _prep_weights(w_in[layer], q_norm[layer], k_norm[layer], rpb[layer], ssm_params, w_glu[layer],
                            w_attn_br[layer], w_out[layer], w_router[layer], w_exp_gate[layer],
                            w_exp_up[layer], w_exp_down[layer])
        y_prompt = _encoder_layer(y_prompt, c_prompt, w_ada[layer], b_ada[layer], norm_mix[layer],
                                  norm_ffn[layer], wts)
        y_sample = _encoder_layer(y_sample, c_sample, w_ada[layer], b_ada[layer], norm_mix[layer],
                                  norm_ffn[layer], wts)
    return (y_prompt, y_sample)
```

```python
import functools
import math

import jax
import jax.numpy as jnp
from jax import lax
from jax.experimental import pallas as pl
from jax.experimental.pallas import tpu as pltpu

F32 = jnp.float32
BF16 = jnp.bfloat16
I32 = jnp.int32

D_MODEL = 1024
GRID_W = 64
N_HEADS = 8
HEAD_DIM = 64
D_ATTN = N_HEADS * HEAD_DIM
WIN_R = 8
WIN_C = 16
SSM_GROUP = 16
D_SSM = 512
N_GROUPS = D_SSM // SSM_GROUP
STATE_P = 64
D_IN = 3 * D_ATTN + D_SSM + 2 * D_MODEL
N_EXPERTS = 16
EC_CAPACITY = 2
D_EXPERT = 2048
EPS = 1e-6
NEG_INF = -1e9

LANES = 128
SUBLANES = 8
BF16_ROWS = 16
VMEM_LIMIT = 56 * 1024 * 1024

SSM_L = 8
SG_GROUPS = LANES // SSM_GROUP
N_SG = N_GROUPS // SG_GROUPS
SG_STATE = SG_GROUPS * STATE_P
SCAN_ROWS = SUBLANES

ATTN_ROWS = 8
ATTN_UNROLL = 2
SLOT_WIN = 128
GATHER_EXPERTS = 4
CUM_W = 256


def _cparams(sem):
    return pltpu.CompilerParams(dimension_semantics=sem, vmem_limit_bytes=VMEM_LIMIT)


def _split_bf16(a):
    hi = a.astype(BF16)
    lo = (a - hi.astype(F32)).astype(BF16)
    return hi, lo


def _dot(a, b):
    return jnp.dot(a, b, preferred_element_type=F32)


def _dot3(a, b):
    ah, al = _split_bf16(a)
    bh, bl = _split_bf16(b)
    return _dot(ah, bh) + (_dot(ah, bl) + _dot(al, bh))


def _sigmoid(z):
    return 1.0 / (1.0 + jnp.exp(-z))


def _ada_kernel(c_ref, w_ref, b_ref, o_ref):
    c = c_ref[...]
    s = c * _sigmoid(c)
    o_ref[...] = _dot3(s, w_ref[...]) + b_ref[...]


def _ada(c_pad, w_ada, b_ada):
    rows = c_pad.shape[0]
    n_out = w_ada.shape[1]
    return pl.pallas_call(
        _ada_kernel,
        grid=(n_out // D_MODEL,),
        in_specs=[pl.BlockSpec((rows, D_MODEL), lambda j: (0, 0)),
                  pl.BlockSpec((D_MODEL, D_MODEL), lambda j: (0, j)),
                  pl.BlockSpec((1, D_MODEL), lambda j: (0, j))],
        out_specs=pl.BlockSpec((rows, D_MODEL), lambda j: (0, j)),
        out_shape=jax.ShapeDtypeStruct((rows, n_out), F32),
        compiler_params=_cparams(("arbitrary",)),
        name="ada",
    )(c_pad, w_ada, b_ada.reshape(1, n_out))


def _inproj_kernel(x_ref, sc_ref, sh_ref, nm_ref, w_ref, qg_ref, kg_ref, ones_ref,
                   q_ref, k_ref, v_ref, u_ref, ga_ref, gs_ref):
    x = x_ref[...]
    ms = jnp.mean(x * x, axis=-1, keepdims=True)
    xn = x * lax.rsqrt(ms + EPS) * nm_ref[...]
    h = (xn * (1.0 + sc_ref[...]) + sh_ref[...]).astype(BF16)

    def proj(lo, hi):
        return _dot(h, w_ref[:, lo:hi])

    def head_norm(z, gain):
        hi, lo = _split_bf16(z * z)
        ssum = _dot(hi, ones_ref[...]) + _dot(lo, ones_ref[...])
        return z * lax.rsqrt(ssum * (1.0 / HEAD_DIM) + EPS) * gain

    q = head_norm(proj(0, D_ATTN), qg_ref[...]) * (HEAD_DIM ** -0.5)
    q_ref[...] = q.astype(BF16)
    k = head_norm(proj(D_ATTN, 2 * D_ATTN), kg_ref[...])
    k_ref[...] = k.astype(BF16)
    v_ref[...] = proj(2 * D_ATTN, 3 * D_ATTN).astype(BF16)
    u = proj(3 * D_ATTN, 3 * D_ATTN + D_SSM)
    for sg in range(N_SG):
        u_ref[sg] = u[:, sg * LANES:(sg + 1) * LANES].astype(BF16)
    o = 3 * D_ATTN + D_SSM
    ga_ref[...] = _sigmoid(proj(o, o + D_MODEL)).astype(BF16)
    gs_ref[...] = _sigmoid(proj(o + D_MODEL, o + 2 * D_MODEL)).astype(BF16)


def _inproj(x, sc1, sh1, norm_mix, w_in_bf, qg, kg, ones_bd, bt):
    b, n, d = x.shape
    tok = lambda w: pl.BlockSpec((None, bt, w), lambda i, j: (i, j, 0))
    mod = pl.BlockSpec((None, 1, d), lambda i, j: (i, 0, 0))
    full = lambda shape: pl.BlockSpec(shape, lambda i, j: tuple(0 for _ in shape))
    return pl.pallas_call(
        _inproj_kernel,
        grid=(b, n // bt),
        in_specs=[tok(d), mod, mod, full((1, d)), full((d, D_IN)),
                  full((1, D_ATTN)), full((1, D_ATTN)), full((D_ATTN, D_ATTN))],
        out_specs=[tok(D_ATTN), tok(D_ATTN), tok(D_ATTN),
                   pl.BlockSpec((None, N_SG, bt, LANES), lambda i, j: (i, 0, j, 0)),
                   tok(d), tok(d)],
        out_shape=[jax.ShapeDtypeStruct((b, n, D_ATTN), BF16)] * 3
        + [jax.ShapeDtypeStruct((b, N_SG, n, LANES), BF16)]
        + [jax.ShapeDtypeStruct((b, n, d), BF16)] * 2,
        compiler_params=_cparams(("parallel", "parallel")),
        name="inproj",
    )(x, sc1, sh1, norm_mix, w_in_bf, qg, kg, ones_bd)


def _attn_kernel(q_ref, kp_ref, kc_ref, kn_ref, vp_ref, vc_ref, vn_ref, bias_ref, o_ref,
                 kwin, vwin, *, rows):
    r0 = pl.program_id(1) * ATTN_ROWS
    rw = ATTN_ROWS * GRID_W
    for t, (kr, vr) in enumerate(((kp_ref, vp_ref), (kc_ref, vc_ref), (kn_ref, vn_ref))):
        kwin[t * rw:(t + 1) * rw, :] = kr[...]
        vwin[t * rw:(t + 1) * rw, :] = vr[...]
    even = lax.broadcasted_iota(I32, (GRID_W, LANES), 1) < HEAD_DIM
    nkeys = WIN_R * GRID_W
    pairs = N_HEADS // 2

    def rows_body(it, carry):
        units = []
        for sub in range(ATTN_UNROLL):
            i = it * ATTN_UNROLL + sub
            r = r0 + i
            rs = jnp.clip(r - WIN_R // 2, 0, rows - WIN_R)
            variant = r - rs
            koff = pl.multiple_of((rs - r0 + ATTN_ROWS) * GRID_W, GRID_W)
            qoff = pl.multiple_of(i * GRID_W, GRID_W)
            for hp in range(pairs):
                ls = slice(hp * LANES, (hp + 1) * LANES)
                qp = q_ref[pl.ds(qoff, GRID_W), ls]
                zero = jnp.zeros_like(qp)
                q2 = jnp.concatenate([jnp.where(even, qp, zero), jnp.where(even, zero, qp)], axis=0)
                kp = kwin[pl.ds(koff, nkeys), ls]
                s = lax.dot_general(q2, kp, (((1,), (1,)), ((), ())), preferred_element_type=F32)
                units.append((qoff, koff, ls, s + bias_ref[variant, hp]))
        probs = []
        for qoff, koff, ls, s in units:
            p = jnp.exp(s - jnp.max(s, axis=-1, keepdims=True))
            probs.append((p.astype(BF16), jnp.sum(p, axis=-1, keepdims=True)))
        for (qoff, koff, ls, _), (p, l) in zip(units, probs):
            o2 = _dot(p, vwin[pl.ds(koff, nkeys), ls]) / l
            o_ref[pl.ds(qoff, GRID_W), ls] = jnp.where(even, o2[:GRID_W], o2[GRID_W:]).astype(BF16)
        return carry

    lax.fori_loop(0, ATTN_ROWS // ATTN_UNROLL, rows_body, 0)


def _attention(q, k, v, bias_tab):
    b, n, _ = q.shape
    rows = n // GRID_W
    assert rows % ATTN_ROWS == 0 and rows >= WIN_R
    nblk = rows // ATTN_ROWS
    rw = ATTN_ROWS * GRID_W
    cur = pl.BlockSpec((None, rw, D_ATTN), lambda i, j: (i, j, 0))
    prv = pl.BlockSpec((None, rw, D_ATTN), lambda i, j: (i, jnp.maximum(j - 1, 0), 0))
    nxt = pl.BlockSpec((None, rw, D_ATTN), lambda i, j: (i, jnp.minimum(j + 1, nblk - 1), 0))
    return pl.pallas_call(
        functools.partial(_attn_kernel, rows=rows),
        grid=(b, nblk),
        in_specs=[cur, prv, cur, nxt, prv, cur, nxt,
                  pl.BlockSpec(bias_tab.shape, lambda i, j: (0, 0, 0, 0))],
        out_specs=cur,
        out_shape=jax.ShapeDtypeStruct((b, n, D_ATTN), BF16),
        scratch_shapes=[pltpu.VMEM((3 * rw, D_ATTN), BF16), pltpu.VMEM((3 * rw, D_ATTN), BF16)],
        compiler_params=_cparams(("parallel", "parallel")),
        name="attn",
    )(q, k, k, k, v, v, v, bias_tab)


def _attn_bias_table(rpb):
    var = jnp.arange(WIN_R)
    a = jnp.arange(WIN_R)
    j = jnp.arange(GRID_W)
    c_start = jnp.clip(j - WIN_C // 2, 0, GRID_W - WIN_C)
    col_ok = (j[None, :] >= c_start[:, None]) & (j[None, :] < c_start[:, None] + WIN_C)
    col_off = jnp.clip(j[None, :] - j[:, None], -(WIN_C - 1), WIN_C - 1) + WIN_C - 1
    row_off = a[None, :] - var[:, None] + WIN_R - 1
    hp = lax.Precision.HIGHEST
    row_sel = (row_off[:, :, None] == jnp.arange(2 * WIN_R - 1)).astype(F32)
    col_sel = (col_off[:, :, None] == jnp.arange(2 * WIN_C - 1)).astype(F32)
    tab = jnp.einsum('hrc,var->hvac', rpb.astype(F32), row_sel, precision=hp)
    tab = jnp.einsum('hvac,jkc->vhjak', tab, col_sel, precision=hp)
    tab = jnp.where(col_ok[None, None, :, None, :], tab, NEG_INF)
    return tab.reshape(WIN_R, N_HEADS // 2, 2 * GRID_W, WIN_R * GRID_W)


def _ssm_tables(a_re, a_im, log_dt, b_re, b_im, c_re, c_im, d_skip):
    L = SSM_L
    hp = lax.Precision.HIGHEST
    lam_re, lam_im = a_re.astype(F32), a_im.astype(F32)
    dt = jnp.exp(log_dt.astype(F32))[..., None]
    ldt_re, ldt_im = lam_re * dt, lam_im * dt

    def apow(kk):
        mag = jnp.exp(ldt_re * kk)
        return mag * jnp.cos(ldt_im * kk), mag * jnp.sin(ldt_im * kk)

    a1_re, a1_im = apow(1.0)
    den = lam_re * lam_re + lam_im * lam_im
    co_re = ((a1_re - 1.0) * lam_re + a1_im * lam_im) / den
    co_im = (a1_im * lam_re - (a1_re - 1.0) * lam_im) / den
    bm_re, bm_im = b_re.astype(F32), b_im.astype(F32)
    bb_re = co_re[..., None] * bm_re - co_im[..., None] * bm_im
    bb_im = co_re[..., None] * bm_im + co_im[..., None] * bm_re
    cm_re, cm_im = c_re.astype(F32), c_im.astype(F32)

    lags = jnp.arange(L, dtype=F32)
    pw_re, pw_im = apow(lags[:, None, None, None])
    e_re = cm_re[None] * pw_re[:, :, :, None, :] - cm_im[None] * pw_im[:, :, :, None, :]
    e_im = cm_re[None] * pw_im[:, :, :, None, :] + cm_im[None] * pw_re[:, :, :, None, :]
    kern = (jnp.einsum('ldgxp,dgpc->ldgxc', e_re, bb_re, precision=hp)
            - jnp.einsum('ldgxp,dgpc->ldgxc', e_im, bb_im, precision=hp))
    s_idx = jnp.arange(L)[:, None]
    t_idx = jnp.arange(L)[None, :]
    lag = jnp.arange(L)
    sel_f = ((t_idx - s_idx)[:, :, None] == lag).astype(F32)
    sel_b = ((s_idx - t_idx)[:, :, None] == lag).astype(F32)
    m = (jnp.einsum('stl,lgxc->gsctx', sel_f, kern[:, 0], precision=hp)
         + jnp.einsum('stl,lgxc->gsctx', sel_b, kern[:, 1], precision=hp))
    eye = jnp.eye(SG_GROUPS, dtype=F32)
    m = m.reshape(N_SG, SG_GROUPS, L, SSM_GROUP, L, SSM_GROUP)
    t_mat = jnp.einsum('qgsctd,gh->qsgcthd', m, eye).reshape(N_SG, L * LANES, L * LANES)

    tau = jnp.arange(L, dtype=F32)[:, None, None]
    parts = []
    for d, expo in ((0, L - 1 - tau), (1, tau)):
        p_re = jnp.exp(ldt_re[d][None] * expo) * jnp.cos(ldt_im[d][None] * expo)
        p_im = jnp.exp(ldt_re[d][None] * expo) * jnp.sin(ldt_im[d][None] * expo)
        br, bi = bb_re[d].transpose(0, 2, 1), bb_im[d].transpose(0, 2, 1)
        parts.append(p_re[:, :, None, :] * br[None] - p_im[:, :, None, :] * bi[None])
        parts.append(p_re[:, :, None, :] * bi[None] + p_im[:, :, None, :] * br[None])
    wb = jnp.stack(parts, axis=0)
    wb = wb.reshape(4, L, N_SG, SG_GROUPS, SSM_GROUP, STATE_P)
    wb_mat = jnp.einsum('mtqgcp,gh->qtgcmhp', wb, eye).reshape(N_SG, L * LANES, 4 * SG_STATE)

    parts = []
    for d, expo in ((0, tau + 1.0), (1, L - tau)):
        p_re = jnp.exp(ldt_re[d][None] * expo) * jnp.cos(ldt_im[d][None] * expo)
        p_im = jnp.exp(ldt_re[d][None] * expo) * jnp.sin(ldt_im[d][None] * expo)
        cr, ci = cm_re[d], cm_im[d]
        z_re = cr[None] * p_re[:, :, None, :] - ci[None] * p_im[:, :, None, :]
        z_im = cr[None] * p_im[:, :, None, :] + ci[None] * p_re[:, :, None, :]
        parts.append(z_re.transpose(1, 3, 0, 2))
        parts.append(-z_im.transpose(1, 3, 0, 2))
    wc = jnp.stack(parts, axis=0).reshape(4, N_SG, SG_GROUPS, STATE_P, L, SSM_GROUP)
    wc_mat = jnp.einsum('mqgptc,gh->qmgpthc', wc, eye).reshape(N_SG, 4 * SG_STATE, L * LANES)

    row = jnp.arange(SCAN_ROWS)
    tiles = []
    for d in range(2):
        kinds = []
        for sh in (1, 2, 4):
            keep = (row >= sh) if d == 0 else (row <= SCAN_ROWS - 1 - sh)
            kinds.append((jnp.full((SCAN_ROWS,), float(L * sh), F32), keep))
        expo = (row + 1.0) if d == 0 else (SCAN_ROWS - row).astype(F32)
        kinds.append((L * expo.astype(F32), jnp.ones((SCAN_ROWS,), bool)))
        per_kind = []
        for expo_r, keep in kinds:
            e = expo_r[:, None, None]
            p_re = jnp.exp(ldt_re[d][None] * e) * jnp.cos(ldt_im[d][None] * e)
            p_im = jnp.exp(ldt_re[d][None] * e) * jnp.sin(ldt_im[d][None] * e)
            k3 = keep[:, None, None]
            per_kind.append(jnp.stack([jnp.where(k3, p_re, 0.0), jnp.where(k3, p_im, 0.0)], axis=0))
        tiles.append(jnp.stack(per_kind, axis=0))
    cst = jnp.stack(tiles, axis=0).reshape(2, 4, 2, SCAN_ROWS, N_SG, SG_STATE)
    cst = cst.transpose(4, 0, 1, 2, 3, 5)

    dsk = jnp.tile(d_skip.astype(F32).reshape(N_SG, 1, LANES), (1, 1, L))
    return t_mat.astype(BF16), wb_mat.astype(BF16), wc_mat.astype(BF16), cst, dsk


def _ssm_kernel(u_ref, t_ref, wb_ref, wc_ref, cst_ref, d_ref, y_ref, st_ref, *, n_chunks, mm_rows):
    n_mm = n_chunks // mm_rows
    n_tiles = n_chunks // SCAN_ROWS
    half = SG_STATE
    row_id = lax.broadcasted_iota(I32, (SCAN_ROWS, half), 0)

    def intra(c, carry):
        rows = pl.ds(pl.multiple_of(c * mm_rows, mm_rows), mm_rows)
        u = u_ref[rows, :]
        y_ref[rows, :] = _dot(u, t_ref[...]) + u.astype(F32) * d_ref[...]
        return carry

    lax.fori_loop(0, n_mm, intra, 0)

    for d in range(2):
        def inject(c, carry, d=d):
            rows = pl.ds(pl.multiple_of(c * mm_rows, mm_rows), mm_rows)
            st_ref[rows, :] = _dot(u_ref[rows, :], wb_ref[:, 2 * d * half:2 * (d + 1) * half])
            return carry

        lax.fori_loop(0, n_mm, inject, 0)

        def scan_tile(i, carry, d=d):
            k = i if d == 0 else n_tiles - 1 - i
            rows = pl.ds(pl.multiple_of(k * SCAN_ROWS, SCAN_ROWS), SCAN_ROWS)
            xr = st_ref[rows, 0:half]
            xi = st_ref[rows, half:2 * half]
            for si, sh in enumerate((1, 2, 4)):
                ar, ai = cst_ref[d, si, 0], cst_ref[d, si, 1]
                shift = sh if d == 0 else SCAN_ROWS - sh
                pr, pi = pltpu.roll(xr, shift, 0), pltpu.roll(xi, shift, 0)
                xr, xi = xr + (ar * pr - ai * pi), xi + (ar * pi + ai * pr)
            cr, ci = carry
            ar, ai = cst_ref[d, 3, 0], cst_ref[d, 3, 1]
            xr, xi = xr + (ar * cr - ai * ci), xi + (ar * ci + ai * cr)
            if d == 0:
                edge, shift, last = 0, 1, SCAN_ROWS - 1
            else:
                edge, shift, last = SCAN_ROWS - 1, SCAN_ROWS - 1, 0
            st_ref[rows, 0:half] = jnp.where(row_id == edge, cr, pltpu.roll(xr, shift, 0))
            st_ref[rows, half:2 * half] = jnp.where(row_id == edge, ci, pltpu.roll(xi, shift, 0))
            return xr[last:last + 1, :], xi[last:last + 1, :]

        zero = jnp.zeros((1, half), F32)
        lax.fori_loop(0, n_tiles, scan_tile, (zero, zero))

        def eject(c, carry, d=d):
            rows = pl.ds(pl.multiple_of(c * mm_rows, mm_rows), mm_rows)
            y_ref[rows, :] += _dot(st_ref[rows, :].astype(BF16), wc_ref[2 * d * half:2 * (d + 1) * half, :])
            return carry

        lax.fori_loop(0, n_mm, eject, 0)


def _ssm(u4, tabs):
    t_mat, wb_mat, wc_mat, cst, dsk = tabs
    b, _, n, _ = u4.shape
    lw = SSM_L * LANES
    n_chunks = n // SSM_L
    mm_rows = min(256, n_chunks)
    assert n_chunks % mm_rows == 0 and n_chunks % SCAN_ROWS == 0
    u = u4.reshape(b, N_SG, n_chunks, lw)
    one = pl.Buffered(1)
    wspec = lambda shape: pl.BlockSpec((None,) + shape, lambda q, i: (q,) + tuple(0 for _ in shape),
                                       pipeline_mode=one)
    y = pl.pallas_call(
        functools.partial(_ssm_kernel, n_chunks=n_chunks, mm_rows=mm_rows),
        grid=(N_SG, b),
        in_specs=[pl.BlockSpec((None, None, n_chunks, lw), lambda q, i: (i, q, 0, 0)),
                  wspec((lw, lw)), wspec((lw, 4 * SG_STATE)), wspec((4 * SG_STATE, lw)),
                  wspec((2, 4, 2, SCAN_ROWS, SG_STATE)), wspec((1, lw))],
        out_specs=pl.BlockSpec((None, None, n_chunks, lw), lambda q, i: (i, q, 0, 0)),
        out_shape=jax.ShapeDtypeStruct((b, N_SG, n_chunks, lw), F32),
        scratch_shapes=[pltpu.VMEM((n_chunks, 2 * SG_STATE), F32)],
        compiler_params=_cparams(("arbitrary", "arbitrary")),
        name="ssm",
    )(u, t_mat, wb_mat, wc_mat, cst, dsk)
    return y.reshape(b, N_SG, n, LANES)


def _merge_kernel(x_ref, attn_ref, y_ref, ga_ref, gs_ref, g1_ref, sc_ref, sh_ref, nf_ref,
                  wab_ref, wglu_ref, wout_ref, wr_ref, x1_ref, h2_ref, lg_ref):
    ab = _dot(attn_ref[...], wab_ref[...])
    y = jnp.concatenate([y_ref[sg] for sg in range(N_SG)], axis=-1)
    gel = 0.5 * y * (1.0 + jnp.tanh(math.sqrt(2.0 / math.pi) * (y + 0.044715 * (y * y * y))))
    glu = _dot(gel.astype(BF16), wglu_ref[...])
    sb = glu[:, :D_MODEL] * _sigmoid(glu[:, D_MODEL:])
    merged = ga_ref[...].astype(F32) * ab + gs_ref[...].astype(F32) * sb
    x1 = x_ref[...] + g1_ref[...] * _dot(merged.astype(BF16), wout_ref[...])
    x1_ref[...] = x1
    ms = jnp.mean(x1 * x1, axis=-1, keepdims=True)
    h2 = x1 * lax.rsqrt(ms + EPS) * nf_ref[...] * (1.0 + sc_ref[...]) + sh_ref[...]
    h2_ref[:, :D_MODEL] = h2.astype(BF16)
    lg = _dot3(h2, wr_ref[...])
    lg_ref[...] = lg
    valid = lax.broadcasted_iota(I32, lg.shape, 1) < N_EXPERTS
    m = jnp.max(jnp.where(valid, lg, -jnp.inf), axis=-1, keepdims=True)
    ex = jnp.where(valid, jnp.exp(lg - m), 0.0)
    aff = ex / jnp.sum(ex, axis=-1, keepdims=True)
    hi = aff.astype(BF16).astype(F32)
    mid = (aff - hi).astype(BF16).astype(F32)
    lo = (aff - hi - mid).astype(BF16).astype(F32)
    parts = hi + pltpu.roll(mid, N_EXPERTS, 1) + pltpu.roll(lo, 2 * N_EXPERTS, 1)
    h2_ref[:, D_MODEL:] = parts.astype(BF16)


def _merge(x, attn, y4, ga, gs, g1, sc2, sh2, norm_ffn, wab, wglu, wout, wr_pad, bt):
    b, n, d = x.shape
    tok = lambda w: pl.BlockSpec((None, bt, w), lambda i, j: (i, j, 0))
    mod = pl.BlockSpec((None, 1, d), lambda i, j: (i, 0, 0))
    full = lambda shape: pl.BlockSpec(shape, lambda i, j: tuple(0 for _ in shape))
    return pl.pallas_call(
        _merge_kernel,
        grid=(b, n // bt),
        in_specs=[tok(d), tok(D_ATTN),
                  pl.BlockSpec((None, N_SG, bt, LANES), lambda i, j: (i, 0, j, 0)),
                  tok(d), tok(d), mod, mod, mod, full((1, d)),
                  full((D_ATTN, d)), full((D_SSM, 2 * d)), full((d, d)), full((d, LANES))],
        out_specs=[tok(d), tok(d + LANES), tok(LANES)],
        out_shape=[jax.ShapeDtypeStruct((b, n, d), F32), jax.ShapeDtypeStruct((b, n, d + LANES), BF16),
                   jax.ShapeDtypeStruct((b, n, LANES), F32)],
        compiler_params=_cparams(("parallel", "parallel")),
        name="merge",
    )(x, attn, y4, ga, gs, g1, sc2, sh2, norm_ffn, wab, wglu, wout, wr_pad)


def _route_kernel(lg_ref, tri_ref, slot_ref, cnt_ref, aff_ref, *, cap):
    lg = lg_ref[...]
    m = jnp.max(lg, axis=1, keepdims=True)
    e = jnp.exp(lg - m)
    aff = e / jnp.sum(e, axis=1, keepdims=True)
    aff_ref[...] = aff

    def count(mask):
        c = jnp.sum(jnp.where(mask, 1.0, 0.0), axis=0, keepdims=True)
        return jnp.sum(c, axis=2, keepdims=True)

    def as_float(bits):
        return lax.bitcast_convert_type(bits, F32)

    def bit_step(i, thr):
        cand = thr | (jnp.int32(1) << (30 - i))
        return jnp.where(count(aff >= as_float(cand)) >= cap, cand, thr)

    thr3 = as_float(lax.fori_loop(0, 31, bit_step, jnp.zeros((1, N_EXPERTS, 1), I32)))
    need = (cap - count(aff > thr3))[0]
    thr = thr3[0]
    n_chunks = lg.shape[0]
    carry0 = jnp.zeros((N_EXPERTS, 1), F32)

    def prefix(flag, carry):
        f = jnp.where(flag, 1.0, 0.0)
        inc = _dot(f.astype(BF16), tri_ref[...]) + carry
        return inc, inc - f

    def tie_body(c, carry):
        a = aff_ref[c]
        tie = a == thr
        inc, rank = prefix(tie, carry)
        sel = (a > thr) | (tie & (rank < need))
        slot_ref[c] = jnp.where(sel, 1, 0).astype(I32)
        return inc[:, CUM_W - 1:CUM_W]

    lax.fori_loop(0, n_chunks, tie_body, carry0)

    def slot_body(c, carry):
        sel = slot_ref[c] > 0
        inc, excl = prefix(sel, carry)
        slot_ref[c] = jnp.where(sel, excl, -1.0).astype(I32)
        cnt_ref[c] = excl.astype(I32)
        return inc[:, CUM_W - 1:CUM_W]

    lax.fori_loop(0, n_chunks, slot_body, carry0)


def _route(logits_c, tri, cap):
    nc = logits_c.shape[0]
    shp = (nc, N_EXPERTS, CUM_W)
    return pl.pallas_call(
        functools.partial(_route_kernel, cap=cap),
        grid=(1,),
        in_specs=[pl.BlockSpec(shp, lambda i: (0, 0, 0)), pl.BlockSpec((CUM_W, CUM_W), lambda i: (0, 0))],
        out_specs=[pl.BlockSpec(shp, lambda i: (0, 0, 0))] * 2,
        out_shape=[jax.ShapeDtypeStruct(shp, I32)] * 2,
        scratch_shapes=[pltpu.VMEM(shp, F32)],
        compiler_params=_cparams(("arbitrary",)),
        name="route",
    )(logits_c, tri)


def _gather_kernel(offs_ref, h_ref, slot_ref, xe_ref, *, nb, cap, win):
    eg = pl.program_id(0)
    blk = pl.program_id(1)

    @pl.when(blk == 0)
    def _():
        xe_ref[...] = jnp.zeros_like(xe_ref)

    riota = lax.broadcasted_iota(I32, (win, 1), 0)

    def window(ee, w):
        off = offs_ref[(eg * GATHER_EXPERTS + ee) * (nb + 1) + blk]
        nominal = (off // BF16_ROWS) * BF16_ROWS + w * win
        start = pl.multiple_of(jnp.minimum(nominal, cap - win), BF16_ROWS)
        slot = slot_ref[ee]
        hit = (slot == riota + start) & (slot >= nominal)
        return jnp.where(hit, 1.0, 0.0).astype(BF16), start

    firsts = [window(ee, 0) for ee in range(GATHER_EXPERTS)]
    picked = _dot(jnp.concatenate([oh for oh, _ in firsts], axis=0), h_ref[...]).astype(BF16)
    for ee, (_, start) in enumerate(firsts):
        xe_ref[ee, pl.ds(start, win), :] += picked[ee * win:(ee + 1) * win]

    for ee in range(GATHER_EXPERTS):
        off = offs_ref[(eg * GATHER_EXPERTS + ee) * (nb + 1) + blk]
        end = offs_ref[(eg * GATHER_EXPERTS + ee) * (nb + 1) + blk + 1]
        n_win = (end - (off // BF16_ROWS) * BF16_ROWS + win - 1) // win

        def extra(w, carry, ee=ee):
            onehot, start = window(ee, w)
            xe_ref[ee, pl.ds(start, win), :] += _dot(onehot, h_ref[...]).astype(BF16)
            return carry

        lax.fori_loop(1, n_win, extra, 0)


def _gather(offs_flat, h2, slot3, cap, bt):
    t, d = h2.shape
    nb = t // bt
    win = min(SLOT_WIN, cap)
    ge = GATHER_EXPERTS
    gs = pltpu.PrefetchScalarGridSpec(
        num_scalar_prefetch=1,
        grid=(N_EXPERTS // ge, nb),
        in_specs=[pl.BlockSpec((bt, d), lambda e, j, offs: (j, 0)),
                  pl.BlockSpec((ge, 1, bt), lambda e, j, offs: (e, 0, j))],
        out_specs=pl.BlockSpec((ge, cap, d), lambda e, j, offs: (e, 0, 0), pipeline_mode=pl.Buffered(1)),
    )
    return pl.pallas_call(
        functools.partial(_gather_kernel, nb=nb, cap=cap, win=win),
        grid_spec=gs,
        out_shape=jax.ShapeDtypeStruct((N_EXPERTS, cap, d), BF16),
        compiler_params=_cparams(("arbitrary", "arbitrary")),
        name="gather",
    )(offs_flat, h2, slot3)


def _ffn_kernel(x_ref, wg_ref, wu_ref, wd_ref, y_ref, *, fchunk):
    x = x_ref[:, :D_MODEL]
    acc = jnp.zeros((x.shape[0], D_MODEL), F32)
    for f in range(D_EXPERT // fchunk):
        fs = slice(f * fchunk, (f + 1) * fchunk)
        a = _dot(x, wg_ref[:, fs])
        u = _dot(x, wu_ref[:, fs])
        hmid = (a * _sigmoid(a) * u).astype(BF16)
        acc = acc + _dot(hmid, wd_ref[fs, :])
    parts = x_ref[:, D_MODEL:].astype(F32)
    lane = lax.broadcasted_iota(I32, parts.shape, 1)
    mine = (lane % N_EXPERTS == pl.program_id(0)) & (lane < 3 * N_EXPERTS)
    gate = jnp.sum(jnp.where(mine, parts, 0.0), axis=-1, keepdims=True)
    y_ref[...] = (gate * acc).astype(BF16)


def _ffn(xe, wg, wu, wd, tm):
    e, cap, dx = xe.shape
    d = D_MODEL
    return pl.pallas_call(
        functools.partial(_ffn_kernel, fchunk=512),
        grid=(e, cap // tm),
        in_specs=[pl.BlockSpec((None, tm, dx), lambda i, j: (i, j, 0)),
                  pl.BlockSpec((None, d, D_EXPERT), lambda i, j: (i, 0, 0)),
                  pl.BlockSpec((None, d, D_EXPERT), lambda i, j: (i, 0, 0)),
                  pl.BlockSpec((None, D_EXPERT, d), lambda i, j: (i, 0, 0))],
        out_specs=pl.BlockSpec((None, tm, d), lambda i, j: (i, j, 0)),
        out_shape=jax.ShapeDtypeStruct((e, cap, d), BF16),
        compiler_params=_cparams(("parallel", "parallel")),
        name="ffn",
    )(xe, wg, wu, wd)


def _window_copy(ye_hbm, buf, sem, e, start, win):
    return pltpu.make_async_copy(ye_hbm.at[e, pl.ds(start, win), :], buf, sem)


def _combine_kernel(offs_ref, x1_ref, slot_ref, g2_ref, ye_hbm, o_ref, ybuf, xbuf, lhs, sems, xsem,
                    *, nb, cap, win):
    blk = pl.program_id(0)
    starts = []
    for e in range(N_EXPERTS):
        off = offs_ref[e * (nb + 1) + blk]
        a0 = (off // BF16_ROWS) * BF16_ROWS
        start = pl.multiple_of(jnp.minimum(a0, cap - win), BF16_ROWS)
        starts.append((a0, start))
        _window_copy(ye_hbm, ybuf.at[pl.ds(e * win, win)], sems.at[e], e, start, win).start()

    liota = lax.broadcasted_iota(I32, (1, win), 1)
    slots = slot_ref[...]
    for e in range(N_EXPERTS):
        hit = slots[:, e:e + 1] == liota + starts[e][1]
        lhs[:, e * win:(e + 1) * win] = jnp.where(hit, 1.0, 0.0).astype(BF16)
    for e in range(N_EXPERTS):
        _window_copy(ye_hbm, ybuf.at[pl.ds(e * win, win)], sems.at[e], e, starts[e][1], win).wait()
    o_ref[...] = x1_ref[...] + g2_ref[...] * _dot(lhs[...], ybuf[...])

    for e in range(N_EXPERTS):
        a0 = starts[e][0]
        end = offs_ref[e * (nb + 1) + blk + 1]
        n_win = (end - a0 + win - 1) // win

        def extra(w, carry, e=e, a0=a0):
            nominal = a0 + w * win
            st = pl.multiple_of(jnp.minimum(nominal, cap - win), BF16_ROWS)
            cp = _window_copy(ye_hbm, xbuf, xsem, e, st, win)
            cp.start()
            cp.wait()
            scol = slot_ref[:, e:e + 1]
            hit = (scol == liota + st) & (scol >= nominal)
            o_ref[...] += g2_ref[...] * _dot(jnp.where(hit, 1.0, 0.0).astype(BF16), xbuf[...])
            return carry

        lax.fori_loop(1, n_win, extra, 0)


def _combine(offs_flat, x1, slot_t, g2, ye, n_per_batch, bt):
    t, d = x1.shape
    nb = t // bt
    cap = ye.shape[1]
    win = min(SLOT_WIN, cap)
    per = n_per_batch // bt
    gs = pltpu.PrefetchScalarGridSpec(
        num_scalar_prefetch=1,
        grid=(nb,),
        in_specs=[pl.BlockSpec((bt, d), lambda j, offs: (j, 0)),
                  pl.BlockSpec((bt, N_EXPERTS), lambda j, offs: (j, 0)),
                  pl.BlockSpec((None, 1, d), lambda j, offs: (j // per, 0, 0)),
                  pl.BlockSpec(memory_space=pl.ANY)],
        out_specs=pl.BlockSpec((bt, d), lambda j, offs: (j, 0)),
        scratch_shapes=[pltpu.VMEM((N_EXPERTS * win, d), BF16), pltpu.VMEM((win, d), BF16),
                        pltpu.VMEM((bt, N_EXPERTS * win), BF16),
                        pltpu.SemaphoreType.DMA((N_EXPERTS,)), pltpu.SemaphoreType.DMA(())],
    )
    return pl.pallas_call(
        functools.partial(_combine_kernel, nb=nb, cap=cap, win=win),
        grid_spec=gs,
        out_shape=jax.ShapeDtypeStruct((t, d), F32),
        compiler_params=_cparams(("arbitrary",)),
        name="combine",
    )(offs_flat, x1, slot_t, g2, ye)


def _prep_weights(w_in, q_norm, k_norm, rpb, ssm_params, w_glu, w_attn_br, w_out, w_router,
                  w_exp_gate, w_exp_up, w_exp_down):
    head = jnp.arange(D_ATTN) // HEAD_DIM
    return dict(
        w_in=w_in.astype(BF16),
        qg=jnp.tile(q_norm.astype(F32), N_HEADS).reshape(1, D_ATTN),
        kg=jnp.tile(k_norm.astype(F32), N_HEADS).reshape(1, D_ATTN),
        ones_bd=(head[:, None] == head[None, :]).astype(BF16),
        bias_tab=_attn_bias_table(rpb),
        ssm=_ssm_tables(*ssm_params),
        wglu=w_glu.astype(BF16), wab=w_attn_br.astype(BF16), wout=w_out.astype(BF16),
        wr=jnp.pad(w_router.astype(F32), ((0, 0), (0, LANES - N_EXPERTS))),
        wg=w_exp_gate.astype(BF16), wu=w_exp_up.astype(BF16), wd=w_exp_down.astype(BF16),
        tri=(jnp.arange(CUM_W)[:, None] <= jnp.arange(CUM_W)[None, :]).astype(BF16),
    )


def _token_block(n, want):
    bt = min(want, n)
    assert n % bt == 0
    return bt


def _encoder_layer(x, c, w_ada, b_ada, norm_mix, norm_ffn, wts):
    b, n, d = x.shape
    t = b * n
    cap = EC_CAPACITY * t // N_EXPERTS

    c_pad = jnp.pad(c.astype(F32), ((0, (-b) % SUBLANES), (0, 0)))
    mod = _ada(c_pad, w_ada, b_ada)[:b]
    sh1, sc1, g1, sh2, sc2, g2 = [m.reshape(b, 1, d) for m in jnp.split(mod, 6, axis=-1)]

    bt = _token_block(n, 512)
    q, k, v, u4, ga, gs = _inproj(x, sc1, sh1, norm_mix.reshape(1, d), wts["w_in"], wts["qg"], wts["kg"],
                                  wts["ones_bd"], bt)
    attn = _attention(q, k, v, wts["bias_tab"])
    y4 = _ssm(u4, wts["ssm"])
    x1, h2, logits = _merge(x, attn, y4, ga, gs, g1, sc2, sh2, norm_ffn.reshape(1, d),
                            wts["wab"], wts["wglu"], wts["wout"], wts["wr"], bt)

    lg = logits.reshape(t, LANES)[:, :N_EXPERTS]
    lg_c = lg.reshape(t // CUM_W, CUM_W, N_EXPERTS).transpose(0, 2, 1)
    slot_c, cnt_c = _route(lg_c, wts["tri"], cap)
    slot_et = slot_c.transpose(1, 0, 2).reshape(N_EXPERTS, t)
    slot_te = slot_et.T

    bt2 = _token_block(t, 512)
    cnt_at_block = cnt_c.transpose(1, 0, 2).reshape(N_EXPERTS, t)[:, ::bt2]
    offs = jnp.concatenate([cnt_at_block, jnp.full((N_EXPERTS, 1), cap, I32)], axis=-1)
    offs_flat = offs.reshape(-1).astype(I32)

    xe = _gather(offs_flat, h2.reshape(t, d + LANES), slot_et.reshape(N_EXPERTS, 1, t), cap, bt2)
    ye = _ffn(xe, wts["wg"], wts["wu"], wts["wd"], _token_block(cap, 1024))
    out = _combine(offs_flat, x1.reshape(t, d), slot_te, g2, ye, n, bt2)
    return out.reshape(b, n, d)


def kernel(x_prompt, x_sample, c_prompt, c_sample, w_ada, b_ada, norm_mix, norm_ffn, w_in, q_norm, k_norm, rpb,
           ssm_a_re, ssm_a_im, ssm_log_dt, ssm_b_re, ssm_b_im, ssm_c_re, ssm_c_im, ssm_d, w_glu, w_attn_br,
           w_out, w_router, w_exp_gate, w_exp_up, w_exp_down):
    y_prompt, y_sample = x_prompt, x_sample
    for layer in range(w_ada.shape[0]):
        ssm_params = tuple(p[layer] for p in (ssm_a_re, ssm_a_im, ssm_log_dt, ssm_b_re, ssm_b_im,
                                              ssm_c_re, ssm_c_im, ssm_d))
        wts = _prep_weights(w_in[layer], q_norm[layer], k_norm[layer], rpb[layer], ssm_params, w_glu[layer],
                            w_attn_br[layer], w_out[layer], w_router[layer], w_exp_gate[layer],
                            w_exp_up[layer], w_exp_down[layer])
        y_prompt = _encoder_layer(y_prompt, c_prompt, w_ada[layer], b_ada[layer], norm_mix[layer],
                                  norm_ffn[layer], wts)
        y_sample = _encoder_layer(y_sample, c_sample, w_ada[layer], b_ada[layer], norm_mix[layer],
                                  norm_ffn[layer], wts)
    return (y_prompt, y_sample)
```

```python
import functools
import math

import jax
import jax.numpy as jnp
from jax import lax
from jax.experimental import pallas as pl
from jax.experimental.pallas import tpu as pltpu

F32 = jnp.float32
BF16 = jnp.bfloat16
I32 = jnp.int32

D_MODEL = 1024
GRID_W = 64
N_HEADS = 8
HEAD_DIM = 64
D_ATTN = N_HEADS * HEAD_DIM
WIN_R = 8
WIN_C = 16
SSM_GROUP = 16
D_SSM = 512
N_GROUPS = D_SSM // SSM_GROUP
STATE_P = 64
D_IN = 3 * D_ATTN + D_SSM + 2 * D_MODEL
N_EXPERTS = 16
EC_CAPACITY = 2
D_EXPERT = 2048
EPS = 1e-6
NEG_INF = -1e9

LANES = 128
SUBLANES = 8
BF16_ROWS = 16
VMEM_LIMIT = 56 * 1024 * 1024

SSM_L = 8
SG_GROUPS = LANES // SSM_GROUP
N_SG = N_GROUPS // SG_GROUPS
SG_STATE = SG_GROUPS * STATE_P
SCAN_ROWS = SUBLANES

ATTN_ROWS = 8
ATTN_UNROLL = 2
SLOT_WIN = 128
GATHER_EXPERTS = 4
CUM_W = 256


def _cparams(sem):
    return pltpu.CompilerParams(dimension_semantics=sem, vmem_limit_bytes=VMEM_LIMIT)


def _split_bf16(a):
    hi = a.astype(BF16)
    lo = (a - hi.astype(F32)).astype(BF16)
    return hi, lo


def _dot(a, b):
    return jnp.dot(a, b, preferred_element_type=F32)


def _dot3(a, b):
    ah, al = _split_bf16(a)
    bh, bl = _split_bf16(b)
    return _dot(ah, bh) + (_dot(ah, bl) + _dot(al, bh))


def _sigmoid(z):
    return 1.0 / (1.0 + jnp.exp(-z))


def _ada_kernel(c_ref, w_ref, b_ref, o_ref):
    c = c_ref[...]
    s = c * _sigmoid(c)
    o_ref[...] = _dot3(s, w_ref[...]) + b_ref[...]


def _ada(c_pad, w_ada, b_ada):
    rows = c_pad.shape[0]
    n_out = w_ada.shape[1]
    return pl.pallas_call(
        _ada_kernel,
        grid=(n_out // D_MODEL,),
        in_specs=[pl.BlockSpec((rows, D_MODEL), lambda j: (0, 0)),
                  pl.BlockSpec((D_MODEL, D_MODEL), lambda j: (0, j)),
                  pl.BlockSpec((1, D_MODEL), lambda j: (0, j))],
        out_specs=pl.BlockSpec((rows, D_MODEL), lambda j: (0, j)),
        out_shape=jax.ShapeDtypeStruct((rows, n_out), F32),
        compiler_params=_cparams(("arbitrary",)),
        name="ada",
    )(c_pad, w_ada, b_ada.reshape(1, n_out))


def _inproj_kernel(x_ref, sc_ref, sh_ref, nm_ref, w_ref, qg_ref, kg_ref, ones_ref,
                   q_ref, k_ref, v_ref, u_ref, ga_ref, gs_ref, h_scr):
    x = x_ref[...]
    ms = jnp.mean(x * x, axis=-1, keepdims=True)
    xn = x * lax.rsqrt(ms + EPS) * nm_ref[...]
    hf = xn * (1.0 + sc_ref[...]) + sh_ref[...]
    for c in range(D_MODEL // LANES):
        h_scr[c] = hf[:, c * LANES:(c + 1) * LANES]
    h = hf.astype(BF16)

    def proj(lo, hi):
        return _dot(h, w_ref[:, lo:hi])

    def head_norm(z, gain):
        hi, lo = _split_bf16(z * z)
        ssum = _dot(hi, ones_ref[...]) + _dot(lo, ones_ref[...])
        return z * lax.rsqrt(ssum * (1.0 / HEAD_DIM) + EPS) * gain

    q = head_norm(proj(0, D_ATTN), qg_ref[...]) * (HEAD_DIM ** -0.5)
    q_ref[...] = q.astype(BF16)
    k = head_norm(proj(D_ATTN, 2 * D_ATTN), kg_ref[...])
    k_ref[...] = k.astype(BF16)
    v_ref[...] = proj(2 * D_ATTN, 3 * D_ATTN).astype(BF16)
    n_rows = x.shape[0] // SSM_L
    hp = jnp.concatenate(
        [jnp.concatenate([h_scr[c, pl.ds(s, n_rows, stride=SSM_L), :] for c in range(D_MODEL // LANES)], axis=1)
         for s in range(SSM_L)], axis=0)
    u = _dot(hp.astype(BF16), w_ref[:, 3 * D_ATTN:3 * D_ATTN + D_SSM])
    for sg in range(N_SG):
        for s in range(SSM_L):
            u_ref[sg, :, s * LANES:(s + 1) * LANES] = (
                u[s * n_rows:(s + 1) * n_rows, sg * LANES:(sg + 1) * LANES].astype(BF16))
    o = 3 * D_ATTN + D_SSM
    ga_ref[...] = _sigmoid(proj(o, o + D_MODEL)).astype(BF16)
    gs_ref[...] = _sigmoid(proj(o + D_MODEL, o + 2 * D_MODEL)).astype(BF16)


def _inproj(x, sc1, sh1, norm_mix, w_in_bf, qg, kg, ones_bd, bt):
    b, n, d = x.shape
    tok = lambda w: pl.BlockSpec((None, bt, w), lambda i, j: (i, j, 0))
    mod = pl.BlockSpec((None, 1, d), lambda i, j: (i, 0, 0))
    full = lambda shape: pl.BlockSpec(shape, lambda i, j: tuple(0 for _ in shape))
    return pl.pallas_call(
        _inproj_kernel,
        grid=(b, n // bt),
        in_specs=[tok(d), mod, mod, full((1, d)), full((d, D_IN)),
                  full((1, D_ATTN)), full((1, D_ATTN)), full((D_ATTN, D_ATTN))],
        out_specs=[tok(D_ATTN), tok(D_ATTN), tok(D_ATTN),
                   pl.BlockSpec((None, N_SG, bt // SSM_L, SSM_L * LANES), lambda i, j: (i, 0, j, 0)),
                   tok(d), tok(d)],
        out_shape=[jax.ShapeDtypeStruct((b, n, D_ATTN), BF16)] * 3
        + [jax.ShapeDtypeStruct((b, N_SG, n // SSM_L, SSM_L * LANES), BF16)]
        + [jax.ShapeDtypeStruct((b, n, d), BF16)] * 2,
        scratch_shapes=[pltpu.VMEM((d // LANES, bt, LANES), F32)],
        compiler_params=_cparams(("parallel", "parallel")),
        name="inproj",
    )(x, sc1, sh1, norm_mix, w_in_bf, qg, kg, ones_bd)


def _attn_kernel(q_ref, kp_ref, kc_ref, kn_ref, vp_ref, vc_ref, vn_ref, bias_ref, o_ref,
                 kwin, vwin, *, rows):
    r0 = pl.program_id(1) * ATTN_ROWS
    rw = ATTN_ROWS * GRID_W
    for t, (kr, vr) in enumerate(((kp_ref, vp_ref), (kc_ref, vc_ref), (kn_ref, vn_ref))):
        kwin[t * rw:(t + 1) * rw, :] = kr[...]
        vwin[t * rw:(t + 1) * rw, :] = vr[...]
    even = lax.broadcasted_iota(I32, (GRID_W, LANES), 1) < HEAD_DIM
    nkeys = WIN_R * GRID_W
    pairs = N_HEADS // 2

    def rows_body(it, carry):
        units = []
        for sub in range(ATTN_UNROLL):
            i = it * ATTN_UNROLL + sub
            r = r0 + i
            rs = jnp.clip(r - WIN_R // 2, 0, rows - WIN_R)
            variant = r - rs
            koff = pl.multiple_of((rs - r0 + ATTN_ROWS) * GRID_W, GRID_W)
            qoff = pl.multiple_of(i * GRID_W, GRID_W)
            for hp in range(pairs):
                ls = slice(hp * LANES, (hp + 1) * LANES)
                qp = q_ref[pl.ds(qoff, GRID_W), ls]
                zero = jnp.zeros_like(qp)
                q2 = jnp.concatenate([jnp.where(even, qp, zero), jnp.where(even, zero, qp)], axis=0)
                kp = kwin[pl.ds(koff, nkeys), ls]
                s = lax.dot_general(q2, kp, (((1,), (1,)), ((), ())), preferred_element_type=F32)
                units.append((qoff, koff, ls, s + bias_ref[variant, hp]))
        probs = []
        for qoff, koff, ls, s in units:
            p = jnp.exp(s - jnp.max(s, axis=-1, keepdims=True))
            probs.append((p.astype(BF16), jnp.sum(p, axis=-1, keepdims=True)))
        for (qoff, koff, ls, _), (p, l) in zip(units, probs):
            o2 = _dot(p, vwin[pl.ds(koff, nkeys), ls]) / l
            o_ref[pl.ds(qoff, GRID_W), ls] = jnp.where(even, o2[:GRID_W], o2[GRID_W:]).astype(BF16)
        return carry

    lax.fori_loop(0, ATTN_ROWS // ATTN_UNROLL, rows_body, 0)


def _attention(q, k, v, bias_tab):
    b, n, _ = q.shape
    rows = n // GRID_W
    assert rows % ATTN_ROWS == 0 and rows >= WIN_R
    nblk = rows // ATTN_ROWS
    rw = ATTN_ROWS * GRID_W
    cur = pl.BlockSpec((None, rw, D_ATTN), lambda i, j: (i, j, 0))
    prv = pl.BlockSpec((None, rw, D_ATTN), lambda i, j: (i, jnp.maximum(j - 1, 0), 0))
    nxt = pl.BlockSpec((None, rw, D_ATTN), lambda i, j: (i, jnp.minimum(j + 1, nblk - 1), 0))
    return pl.pallas_call(
        functools.partial(_attn_kernel, rows=rows),
        grid=(b, nblk),
        in_specs=[cur, prv, cur, nxt, prv, cur, nxt,
                  pl.BlockSpec(bias_tab.shape, lambda i, j: (0, 0, 0, 0))],
        out_specs=cur,
        out_shape=jax.ShapeDtypeStruct((b, n, D_ATTN), BF16),
        scratch_shapes=[pltpu.VMEM((3 * rw, D_ATTN), BF16), pltpu.VMEM((3 * rw, D_ATTN), BF16)],
        compiler_params=_cparams(("parallel", "parallel")),
        name="attn",
    )(q, k, k, k, v, v, v, bias_tab)


def _attn_bias_table(rpb):
    var = jnp.arange(WIN_R)
    a = jnp.arange(WIN_R)
    j = jnp.arange(GRID_W)
    c_start = jnp.clip(j - WIN_C // 2, 0, GRID_W - WIN_C)
    col_ok = (j[None, :] >= c_start[:, None]) & (j[None, :] < c_start[:, None] + WIN_C)
    col_off = jnp.clip(j[None, :] - j[:, None], -(WIN_C - 1), WIN_C - 1) + WIN_C - 1
    row_off = a[None, :] - var[:, None] + WIN_R - 1
    hp = lax.Precision.HIGHEST
    row_sel = (row_off[:, :, None] == jnp.arange(2 * WIN_R - 1)).astype(F32)
    col_sel = (col_off[:, :, None] == jnp.arange(2 * WIN_C - 1)).astype(F32)
    tab = jnp.einsum('hrc,var->hvac', rpb.astype(F32), row_sel, precision=hp)
    tab = jnp.einsum('hvac,jkc->vhjak', tab, col_sel, precision=hp)
    tab = jnp.where(col_ok[None, None, :, None, :], tab, NEG_INF)
    return tab.reshape(WIN_R, N_HEADS // 2, 2 * GRID_W, WIN_R * GRID_W)


def _ssm_tables(a_re, a_im, log_dt, b_re, b_im, c_re, c_im, d_skip):
    L = SSM_L
    hp = lax.Precision.HIGHEST
    lam_re, lam_im = a_re.astype(F32), a_im.astype(F32)
    dt = jnp.exp(log_dt.astype(F32))[..., None]
    ldt_re, ldt_im = lam_re * dt, lam_im * dt

    def apow(kk):
        mag = jnp.exp(ldt_re * kk)
        return mag * jnp.cos(ldt_im * kk), mag * jnp.sin(ldt_im * kk)

    a1_re, a1_im = apow(1.0)
    den = lam_re * lam_re + lam_im * lam_im
    co_re = ((a1_re - 1.0) * lam_re + a1_im * lam_im) / den
    co_im = (a1_im * lam_re - (a1_re - 1.0) * lam_im) / den
    bm_re, bm_im = b_re.astype(F32), b_im.astype(F32)
    bb_re = co_re[..., None] * bm_re - co_im[..., None] * bm_im
    bb_im = co_re[..., None] * bm_im + co_im[..., None] * bm_re
    cm_re, cm_im = c_re.astype(F32), c_im.astype(F32)

    lags = jnp.arange(L, dtype=F32)
    pw_re, pw_im = apow(lags[:, None, None, None])
    e_re = cm_re[None] * pw_re[:, :, :, None, :] - cm_im[None] * pw_im[:, :, :, None, :]
    e_im = cm_re[None] * pw_im[:, :, :, None, :] + cm_im[None] * pw_re[:, :, :, None, :]
    kern = (jnp.einsum('ldgxp,dgpc->ldgxc', e_re, bb_re, precision=hp)
            - jnp.einsum('ldgxp,dgpc->ldgxc', e_im, bb_im, precision=hp))
    s_idx = jnp.arange(L)[:, None]
    t_idx = jnp.arange(L)[None, :]
    lag = jnp.arange(L)
    sel_f = ((t_idx - s_idx)[:, :, None] == lag).astype(F32)
    sel_b = ((s_idx - t_idx)[:, :, None] == lag).astype(F32)
    m = (jnp.einsum('stl,lgxc->gsctx', sel_f, kern[:, 0], precision=hp)
         + jnp.einsum('stl,lgxc->gsctx', sel_b, kern[:, 1], precision=hp))
    eye = jnp.eye(SG_GROUPS, dtype=F32)
    m = m.reshape(N_SG, SG_GROUPS, L, SSM_GROUP, L, SSM_GROUP)
    t_mat = jnp.einsum('qgsctd,gh->qsgcthd', m, eye).reshape(N_SG, L * LANES, L * LANES)

    tau = jnp.arange(L, dtype=F32)[:, None, None]
    parts = []
    for d, expo in ((0, L - 1 - tau), (1, tau)):
        p_re = jnp.exp(ldt_re[d][None] * expo) * jnp.cos(ldt_im[d][None] * expo)
        p_im = jnp.exp(ldt_re[d][None] * expo) * jnp.sin(ldt_im[d][None] * expo)
        br, bi = bb_re[d].transpose(0, 2, 1), bb_im[d].transpose(0, 2, 1)
        parts.append(p_re[:, :, None, :] * br[None] - p_im[:, :, None, :] * bi[None])
        parts.append(p_re[:, :, None, :] * bi[None] + p_im[:, :, None, :] * br[None])
    wb = jnp.stack(parts, axis=0)
    wb = wb.reshape(4, L, N_SG, SG_GROUPS, SSM_GROUP, STATE_P)
    wb_mat = jnp.einsum('mtqgcp,gh->qtgcmhp', wb, eye).reshape(N_SG, L * LANES, 4 * SG_STATE)

    parts = []
    for d, expo in ((0, tau + 1.0), (1, L - tau)):
        p_re = jnp.exp(ldt_re[d][None] * expo) * jnp.cos(ldt_im[d][None] * expo)
        p_im = jnp.exp(ldt_re[d][None] * expo) * jnp.sin(ldt_im[d][None] * expo)
        cr, ci = cm_re[d], cm_im[d]
        z_re = cr[None] * p_re[:, :, None, :] - ci[None] * p_im[:, :, None, :]
        z_im = cr[None] * p_im[:, :, None, :] + ci[None] * p_re[:, :, None, :]
        parts.append(z_re.transpose(1, 3, 0, 2))
        parts.append(-z_im.transpose(1, 3, 0, 2))
    wc = jnp.stack(parts, axis=0).reshape(4, N_SG, SG_GROUPS, STATE_P, L, SSM_GROUP)
    wc_mat = jnp.einsum('mqgptc,gh->qmgpthc', wc, eye).reshape(N_SG, 4 * SG_STATE, L * LANES)

    row = jnp.arange(SCAN_ROWS)
    tiles = []
    for d in range(2):
        kinds = []
        for sh in (1, 2, 4):
            keep = (row >= sh) if d == 0 else (row <= SCAN_ROWS - 1 - sh)
            kinds.append((jnp.full((SCAN_ROWS,), float(L * sh), F32), keep))
        expo = (row + 1.0) if d == 0 else (SCAN_ROWS - row).astype(F32)
        kinds.append((L * expo.astype(F32), jnp.ones((SCAN_ROWS,), bool)))
        per_kind = []
        for expo_r, keep in kinds:
            e = expo_r[:, None, None]
            p_re = jnp.exp(ldt_re[d][None] * e) * jnp.cos(ldt_im[d][None] * e)
            p_im = jnp.exp(ldt_re[d][None] * e) * jnp.sin(ldt_im[d][None] * e)
            k3 = keep[:, None, None]
            per_kind.append(jnp.stack([jnp.where(k3, p_re, 0.0), jnp.where(k3, p_im, 0.0)], axis=0))
        tiles.append(jnp.stack(per_kind, axis=0))
    cst = jnp.stack(tiles, axis=0).reshape(2, 4, 2, SCAN_ROWS, N_SG, SG_STATE)
    cst = cst.transpose(4, 0, 1, 2, 3, 5)

    dsk = jnp.tile(d_skip.astype(F32).reshape(N_SG, 1, LANES), (1, 1, L))
    return t_mat.astype(BF16), wb_mat.astype(BF16), wc_mat.astype(BF16), cst, dsk


def _ssm_kernel(u_ref, t_ref, wb_ref, wc_ref, cst_ref, d_ref, y_ref, st_ref, *, n_chunks, mm_rows):
    n_mm = n_chunks // mm_rows
    n_tiles = n_chunks // SCAN_ROWS
    half = SG_STATE
    row_id = lax.broadcasted_iota(I32, (SCAN_ROWS, half), 0)

    def intra(c, carry):
        rows = pl.ds(pl.multiple_of(c * mm_rows, mm_rows), mm_rows)
        u = u_ref[rows, :]
        y_ref[rows, :] = _dot(u, t_ref[...]) + u.astype(F32) * d_ref[...]
        return carry

    lax.fori_loop(0, n_mm, intra, 0)

    for d in range(2):
        def inject(c, carry, d=d):
            rows = pl.ds(pl.multiple_of(c * mm_rows, mm_rows), mm_rows)
            st_ref[rows, :] = _dot(u_ref[rows, :], wb_ref[:, 2 * d * half:2 * (d + 1) * half])
            return carry

        lax.fori_loop(0, n_mm, inject, 0)

        def scan_tile(i, carry, d=d):
            k = i if d == 0 else n_tiles - 1 - i
            rows = pl.ds(pl.multiple_of(k * SCAN_ROWS, SCAN_ROWS), SCAN_ROWS)
            xr = st_ref[rows, 0:half]
            xi = st_ref[rows, half:2 * half]
            for si, sh in enumerate((1, 2, 4)):
                ar, ai = cst_ref[d, si, 0], cst_ref[d, si, 1]
                shift = sh if d == 0 else SCAN_ROWS - sh
                pr, pi = pltpu.roll(xr, shift, 0), pltpu.roll(xi, shift, 0)
                xr, xi = xr + (ar * pr - ai * pi), xi + (ar * pi + ai * pr)
            cr, ci = carry
            ar, ai = cst_ref[d, 3, 0], cst_ref[d, 3, 1]
            xr, xi = xr + (ar * cr - ai * ci), xi + (ar * ci + ai * cr)
            if d == 0:
                edge, shift, last = 0, 1, SCAN_ROWS - 1
            else:
                edge, shift, last = SCAN_ROWS - 1, SCAN_ROWS - 1, 0
            st_ref[rows, 0:half] = jnp.where(row_id == edge, cr, pltpu.roll(xr, shift, 0))
            st_ref[rows, half:2 * half] = jnp.where(row_id == edge, ci, pltpu.roll(xi, shift, 0))
            return xr[last:last + 1, :], xi[last:last + 1, :]

        zero = jnp.zeros((1, half), F32)
        lax.fori_loop(0, n_tiles, scan_tile, (zero, zero))

        def eject(c, carry, d=d):
            rows = pl.ds(pl.multiple_of(c * mm_rows, mm_rows), mm_rows)
            y_ref[rows, :] += _dot(st_ref[rows, :].astype(BF16), wc_ref[2 * d * half:2 * (d + 1) * half, :])
            return carry

        lax.fori_loop(0, n_mm, eject, 0)


def _ssm(u, tabs):
    t_mat, wb_mat, wc_mat, cst, dsk = tabs
    b, _, n_chunks, lw = u.shape
    mm_rows = min(256, n_chunks)
    assert n_chunks % mm_rows == 0 and n_chunks % SCAN_ROWS == 0
    one = pl.Buffered(1)
    wspec = lambda shape: pl.BlockSpec((None,) + shape, lambda q, i: (q,) + tuple(0 for _ in shape),
                                       pipeline_mode=one)
    return pl.pallas_call(
        functools.partial(_ssm_kernel, n_chunks=n_chunks, mm_rows=mm_rows),
        grid=(N_SG, b),
        in_specs=[pl.BlockSpec((None, None, n_chunks, lw), lambda q, i: (i, q, 0, 0)),
                  wspec((lw, lw)), wspec((lw, 4 * SG_STATE)), wspec((4 * SG_STATE, lw)),
                  wspec((2, 4, 2, SCAN_ROWS, SG_STATE)), wspec((1, lw))],
        out_specs=pl.BlockSpec((None, None, n_chunks, lw), lambda q, i: (i, q, 0, 0)),
        out_shape=jax.ShapeDtypeStruct((b, N_SG, n_chunks, lw), F32),
        scratch_shapes=[pltpu.VMEM((n_chunks, 2 * SG_STATE), F32)],
        compiler_params=_cparams(("arbitrary", "arbitrary")),
        name="ssm",
    )(u, t_mat, wb_mat, wc_mat, cst, dsk)


def _merge_kernel(x_ref, attn_ref, y_ref, ga_ref, gs_ref, g1_ref, sc_ref, sh_ref, nf_ref,
                  wab_ref, wglu_ref, wout_ref, wr_ref, x1_ref, h2_ref, lg_ref, gel_scr):
    ab = _dot(attn_ref[...], wab_ref[...])
    n_rows = x_ref.shape[0] // SSM_L
    for sg in range(N_SG):
        for s in range(SSM_L):
            y = y_ref[sg, :, s * LANES:(s + 1) * LANES]
            gel_scr[sg, s * n_rows:(s + 1) * n_rows, :] = (
                0.5 * y * (1.0 + jnp.tanh(math.sqrt(2.0 / math.pi) * (y + 0.044715 * (y * y * y)))))
    gel = jnp.concatenate(
        [jnp.concatenate([gel_scr[sg, pl.ds(j, SSM_L, stride=n_rows), :] for j in range(n_rows)], axis=0)
         for sg in range(N_SG)], axis=1)
    glu = _dot(gel.astype(BF16), wglu_ref[...])
    sb = glu[:, :D_MODEL] * _sigmoid(glu[:, D_MODEL:])
    merged = ga_ref[...].astype(F32) * ab + gs_ref[...].astype(F32) * sb
    x1 = x_ref[...] + g1_ref[...] * _dot(merged.astype(BF16), wout_ref[...])
    x1_ref[...] = x1
    ms = jnp.mean(x1 * x1, axis=-1, keepdims=True)
    h2 = x1 * lax.rsqrt(ms + EPS) * nf_ref[...] * (1.0 + sc_ref[...]) + sh_ref[...]
    h2_ref[:, :D_MODEL] = h2.astype(BF16)
    lg = _dot3(h2, wr_ref[...])
    lg_ref[...] = lg
    valid = lax.broadcasted_iota(I32, lg.shape, 1) < N_EXPERTS
    m = jnp.max(jnp.where(valid, lg, -jnp.inf), axis=-1, keepdims=True)
    ex = jnp.where(valid, jnp.exp(lg - m), 0.0)
    aff = ex / jnp.sum(ex, axis=-1, keepdims=True)
    hi = aff.astype(BF16).astype(F32)
    mid = (aff - hi).astype(BF16).astype(F32)
    lo = (aff - hi - mid).astype(BF16).astype(F32)
    parts = hi + pltpu.roll(mid, N_EXPERTS, 1) + pltpu.roll(lo, 2 * N_EXPERTS, 1)
    h2_ref[:, D_MODEL:] = parts.astype(BF16)


def _merge(x, attn, y4, ga, gs, g1, sc2, sh2, norm_ffn, wab, wglu, wout, wr_pad, bt):
    b, n, d = x.shape
    tok = lambda w: pl.BlockSpec((None, bt, w), lambda i, j: (i, j, 0))
    mod = pl.BlockSpec((None, 1, d), lambda i, j: (i, 0, 0))
    full = lambda shape: pl.BlockSpec(shape, lambda i, j: tuple(0 for _ in shape))
    return pl.pallas_call(
        _merge_kernel,
        grid=(b, n // bt),
        in_specs=[tok(d), tok(D_ATTN),
                  pl.BlockSpec((None, N_SG, bt // SSM_L, SSM_L * LANES), lambda i, j: (i, 0, j, 0)),
                  tok(d), tok(d), mod, mod, mod, full((1, d)),
                  full((D_ATTN, d)), full((D_SSM, 2 * d)), full((d, d)), full((d, LANES))],
        out_specs=[tok(d), tok(d + LANES), tok(LANES)],
        out_shape=[jax.ShapeDtypeStruct((b, n, d), F32), jax.ShapeDtypeStruct((b, n, d + LANES), BF16),
                   jax.ShapeDtypeStruct((b, n, LANES), F32)],
        scratch_shapes=[pltpu.VMEM((N_SG, bt, LANES), F32)],
        compiler_params=_cparams(("parallel", "parallel")),
        name="merge",
    )(x, attn, y4, ga, gs, g1, sc2, sh2, norm_ffn, wab, wglu, wout, wr_pad)


def _route_kernel(lg_ref, tri_ref, slot_ref, cnt_ref, aff_ref, *, cap):
    lg = lg_ref[...]
    m = jnp.max(lg, axis=1, keepdims=True)
    e = jnp.exp(lg - m)
    aff = e / jnp.sum(e, axis=1, keepdims=True)
    aff_ref[...] = aff

    def count(mask):
        c = jnp.sum(jnp.where(mask, 1.0, 0.0), axis=0, keepdims=True)
        return jnp.sum(c, axis=2, keepdims=True)

    def as_float(bits):
        return lax.bitcast_convert_type(bits, F32)

    def bit_step(i, thr):
        cand = thr | (jnp.int32(1) << (30 - i))
        return jnp.where(count(aff >= as_float(cand)) >= cap, cand, thr)

    thr3 = as_float(lax.fori_loop(0, 31, bit_step, jnp.zeros((1, N_EXPERTS, 1), I32)))
    need = (cap - count(aff > thr3))[0]
    thr = thr3[0]
    n_chunks = lg.shape[0]
    carry0 = jnp.zeros((N_EXPERTS, 1), F32)

    def prefix(flag, carry):
        f = jnp.where(flag, 1.0, 0.0)
        inc = _dot(f.astype(BF16), tri_ref[...]) + carry
        return inc, inc - f

    def tie_body(c, carry):
        a = aff_ref[c]
        tie = a == thr
        inc, rank = prefix(tie, carry)
        sel = (a > thr) | (tie & (rank < need))
        slot_ref[c] = jnp.where(sel, 1, 0).astype(I32)
        return inc[:, CUM_W - 1:CUM_W]

    lax.fori_loop(0, n_chunks, tie_body, carry0)

    def slot_body(c, carry):
        sel = slot_ref[c] > 0
        inc, excl = prefix(sel, carry)
        slot_ref[c] = jnp.where(sel, excl, -1.0).astype(I32)
        cnt_ref[c] = excl.astype(I32)
        return inc[:, CUM_W - 1:CUM_W]

    lax.fori_loop(0, n_chunks, slot_body, carry0)


def _route(logits_c, tri, cap):
    nc = logits_c.shape[0]
    shp = (nc, N_EXPERTS, CUM_W)
    return pl.pallas_call(
        functools.partial(_route_kernel, cap=cap),
        grid=(1,),
        in_specs=[pl.BlockSpec(shp, lambda i: (0, 0, 0)), pl.BlockSpec((CUM_W, CUM_W), lambda i: (0, 0))],
        out_specs=[pl.BlockSpec(shp, lambda i: (0, 0, 0))] * 2,
        out_shape=[jax.ShapeDtypeStruct(shp, I32)] * 2,
        scratch_shapes=[pltpu.VMEM(shp, F32)],
        compiler_params=_cparams(("arbitrary",)),
        name="route",
    )(logits_c, tri)


def _gather_kernel(offs_ref, h_ref, slot_ref, xe_ref, *, nb, cap, win):
    eg = pl.program_id(0)
    blk = pl.program_id(1)

    @pl.when(blk == 0)
    def _():
        xe_ref[...] = jnp.zeros_like(xe_ref)

    riota = lax.broadcasted_iota(I32, (win, 1), 0)

    def window(ee, w):
        off = offs_ref[(eg * GATHER_EXPERTS + ee) * (nb + 1) + blk]
        nominal = (off // BF16_ROWS) * BF16_ROWS + w * win
        start = pl.multiple_of(jnp.minimum(nominal, cap - win), BF16_ROWS)
        slot = slot_ref[ee]
        hit = (slot == riota + start) & (slot >= nominal)
        return jnp.where(hit, 1.0, 0.0).astype(BF16), start

    firsts = [window(ee, 0) for ee in range(GATHER_EXPERTS)]
    picked = _dot(jnp.concatenate([oh for oh, _ in firsts], axis=0), h_ref[...]).astype(BF16)
    for ee, (_, start) in enumerate(firsts):
        xe_ref[ee, pl.ds(start, win), :] += picked[ee * win:(ee + 1) * win]

    for ee in range(GATHER_EXPERTS):
        off = offs_ref[(eg * GATHER_EXPERTS + ee) * (nb + 1) + blk]
        end = offs_ref[(eg * GATHER_EXPERTS + ee) * (nb + 1) + blk + 1]
        n_win = (end - (off // BF16_ROWS) * BF16_ROWS + win - 1) // win

        def extra(w, carry, ee=ee):
            onehot, start = window(ee, w)
            xe_ref[ee, pl.ds(start, win), :] += _dot(onehot, h_ref[...]).astype(BF16)
            return carry

        lax.fori_loop(1, n_win, extra, 0)


def _gather(offs_flat, h2, slot3, cap, bt):
    t, d = h2.shape
    nb = t // bt
    win = min(SLOT_WIN, cap)
    ge = GATHER_EXPERTS
    gs = pltpu.PrefetchScalarGridSpec(
        num_scalar_prefetch=1,
        grid=(N_EXPERTS // ge, nb),
        in_specs=[pl.BlockSpec((bt, d), lambda e, j, offs: (j, 0)),
                  pl.BlockSpec((ge, 1, bt), lambda e, j, offs: (e, 0, j))],
        out_specs=pl.BlockSpec((ge, cap, d), lambda e, j, offs: (e, 0, 0), pipeline_mode=pl.Buffered(1)),
    )
    return pl.pallas_call(
        functools.partial(_gather_kernel, nb=nb, cap=cap, win=win),
        grid_spec=gs,
        out_shape=jax.ShapeDtypeStruct((N_EXPERTS, cap, d), BF16),
        compiler_params=_cparams(("arbitrary", "arbitrary")),
        name="gather",
    )(offs_flat, h2, slot3)


def _ffn_kernel(x_ref, wg_ref, wu_ref, wd_ref, y_ref, *, fchunk):
    x = x_ref[:, :D_MODEL]
    acc = jnp.zeros((x.shape[0], D_MODEL), F32)
    for f in range(D_EXPERT // fchunk):
        fs = slice(f * fchunk, (f + 1) * fchunk)
        a = _dot(x, wg_ref[:, fs])
        u = _dot(x, wu_ref[:, fs])
        hmid = (a * _sigmoid(a) * u).astype(BF16)
        acc = acc + _dot(hmid, wd_ref[fs, :])
    parts = x_ref[:, D_MODEL:].astype(F32)
    lane = lax.broadcasted_iota(I32, parts.shape, 1)
    mine = (lane % N_EXPERTS == pl.program_id(0)) & (lane < 3 * N_EXPERTS)
    gate = jnp.sum(jnp.where(mine, parts, 0.0), axis=-1, keepdims=True)
    y_ref[...] = (gate * acc).astype(BF16)


def _ffn(xe, wg, wu, wd, tm):
    e, cap, dx = xe.shape
    d = D_MODEL
    return pl.pallas_call(
        functools.partial(_ffn_kernel, fchunk=512),
        grid=(e, cap // tm),
        in_specs=[pl.BlockSpec((None, tm, dx), lambda i, j: (i, j, 0)),
                  pl.BlockSpec((None, d, D_EXPERT), lambda i, j: (i, 0, 0)),
                  pl.BlockSpec((None, d, D_EXPERT), lambda i, j: (i, 0, 0)),
                  pl.BlockSpec((None, D_EXPERT, d), lambda i, j: (i, 0, 0))],
        out_specs=pl.BlockSpec((None, tm, d), lambda i, j: (i, j, 0)),
        out_shape=jax.ShapeDtypeStruct((e, cap, d), BF16),
        compiler_params=_cparams(("parallel", "parallel")),
        name="ffn",
    )(xe, wg, wu, wd)


def _window_copy(ye_hbm, buf, sem, e, start, win):
    return pltpu.make_async_copy(ye_hbm.at[e, pl.ds(start, win), :], buf, sem)


def _combine_kernel(offs_ref, x1_ref, slot_ref, g2_ref, ye_hbm, o_ref, ybuf, xbuf, lhs, sems, xsem,
                    *, nb, cap, win):
    blk = pl.program_id(0)

    def first_window(b_, e):
        off = offs_ref[e * (nb + 1) + b_]
        a0 = (off // BF16_ROWS) * BF16_ROWS
        return a0, pl.multiple_of(jnp.minimum(a0, cap - win), BF16_ROWS)

    def window_copies(b_, half):
        return [_window_copy(ye_hbm, ybuf.at[half, pl.ds(e * win, win)], sems.at[half, e], e,
                             first_window(b_, e)[1], win) for e in range(N_EXPERTS)]

    @pl.when(blk == 0)
    def _():
        for cp in window_copies(0, 0):
            cp.start()

    @pl.when(blk + 1 < nb)
    def _():
        for cp in window_copies(blk + 1, (blk + 1) % 2):
            cp.start()

    starts = [first_window(blk, e) for e in range(N_EXPERTS)]
    liota = lax.broadcasted_iota(I32, (1, win), 1)
    slots = slot_ref[...]
    for e in range(N_EXPERTS):
        hit = slots[:, e:e + 1] == liota + starts[e][1]
        lhs[:, e * win:(e + 1) * win] = jnp.where(hit, 1.0, 0.0).astype(BF16)
    for cp in window_copies(blk, blk % 2):
        cp.wait()
    o_ref[...] = x1_ref[...] + g2_ref[...] * _dot(lhs[...], ybuf[blk % 2])

    for e in range(N_EXPERTS):
        a0 = starts[e][0]
        end = offs_ref[e * (nb + 1) + blk + 1]
        n_win = (end - a0 + win - 1) // win

        def extra(w, carry, e=e, a0=a0):
            nominal = a0 + w * win
            st = pl.multiple_of(jnp.minimum(nominal, cap - win), BF16_ROWS)
            cp = _window_copy(ye_hbm, xbuf, xsem, e, st, win)
            cp.start()
            cp.wait()
            scol = slot_ref[:, e:e + 1]
            hit = (scol == liota + st) & (scol >= nominal)
            o_ref[...] += g2_ref[...] * _dot(jnp.where(hit, 1.0, 0.0).astype(BF16), xbuf[...])
            return carry

        lax.fori_loop(1, n_win, extra, 0)


def _combine(offs_flat, x1, slot_t, g2, ye, n_per_batch, bt):
    t, d = x1.shape
    nb = t // bt
    cap = ye.shape[1]
    win = min(SLOT_WIN, cap)
    per = n_per_batch // bt
    gs = pltpu.PrefetchScalarGridSpec(
        num_scalar_prefetch=1,
        grid=(nb,),
        in_specs=[pl.BlockSpec((bt, d), lambda j, offs: (j, 0)),
                  pl.BlockSpec((bt, N_EXPERTS), lambda j, offs: (j, 0)),
                  pl.BlockSpec((None, 1, d), lambda j, offs: (j // per, 0, 0)),
                  pl.BlockSpec(memory_space=pl.ANY)],
        out_specs=pl.BlockSpec((bt, d), lambda j, offs: (j, 0)),
        scratch_shapes=[pltpu.VMEM((2, N_EXPERTS * win, d), BF16), pltpu.VMEM((win, d), BF16),
                        pltpu.VMEM((bt, N_EXPERTS * win), BF16),
                        pltpu.SemaphoreType.DMA((2, N_EXPERTS)), pltpu.SemaphoreType.DMA(())],
    )
    return pl.pallas_call(
        functools.partial(_combine_kernel, nb=nb, cap=cap, win=win),
        grid_spec=gs,
        out_shape=jax.ShapeDtypeStruct((t, d), F32),
        compiler_params=_cparams(("arbitrary",)),
        name="combine",
    )(offs_flat, x1, slot_t, g2, ye)


def _prep_weights(w_in, q_norm, k_norm, rpb, ssm_params, w_glu, w_attn_br, w_out, w_router,
                  w_exp_gate, w_exp_up, w_exp_down):
    head = jnp.arange(D_ATTN) // HEAD_DIM
    return dict(
        w_in=w_in.astype(BF16),
        qg=jnp.tile(q_norm.astype(F32), N_HEADS).reshape(1, D_ATTN),
        kg=jnp.tile(k_norm.astype(F32), N_HEADS).reshape(1, D_ATTN),
        ones_bd=(head[:, None] == head[None, :]).astype(BF16),
        bias_tab=_attn_bias_table(rpb),
        ssm=_ssm_tables(*ssm_params),
        wglu=w_glu.astype(BF16), wab=w_attn_br.astype(BF16), wout=w_out.astype(BF16),
        wr=jnp.pad(w_router.astype(F32), ((0, 0), (0, LANES - N_EXPERTS))),
        wg=w_exp_gate.astype(BF16), wu=w_exp_up.astype(BF16), wd=w_exp_down.astype(BF16),
        tri=(jnp.arange(CUM_W)[:, None] <= jnp.arange(CUM_W)[None, :]).astype(BF16),
    )


def _token_block(n, want):
    bt = min(want, n)
    assert n % bt == 0
    return bt


def _encoder_layer(x, c, w_ada, b_ada, norm_mix, norm_ffn, wts):
    b, n, d = x.shape
    t = b * n
    cap = EC_CAPACITY * t // N_EXPERTS

    c_pad = jnp.pad(c.astype(F32), ((0, (-b) % SUBLANES), (0, 0)))
    mod = _ada(c_pad, w_ada, b_ada)[:b]
    sh1, sc1, g1, sh2, sc2, g2 = [m.reshape(b, 1, d) for m in jnp.split(mod, 6, axis=-1)]

    bt = _token_block(n, 512)
    q, k, v, u4, ga, gs = _inproj(x, sc1, sh1, norm_mix.reshape(1, d), wts["w_in"], wts["qg"], wts["kg"],
                                  wts["ones_bd"], bt)
    attn = _attention(q, k, v, wts["bias_tab"])
    y4 = _ssm(u4, wts["ssm"])
    x1, h2, logits = _merge(x, attn, y4, ga, gs, g1, sc2, sh2, norm_ffn.reshape(1, d),
                            wts["wab"], wts["wglu"], wts["wout"], wts["wr"], bt)

    lg = logits.reshape(t, LANES)[:, :N_EXPERTS]
    lg_c = lg.reshape(t // CUM_W, CUM_W, N_EXPERTS).transpose(0, 2, 1)
    slot_c, cnt_c = _route(lg_c, wts["tri"], cap)
    slot_et = slot_c.transpose(1, 0, 2).reshape(N_EXPERTS, t)
    slot_te = slot_et.T

    bt2 = _token_block(t, 512)
    cnt_at_block = cnt_c.transpose(1, 0, 2).reshape(N_EXPERTS, t)[:, ::bt2]
    offs = jnp.concatenate([cnt_at_block, jnp.full((N_EXPERTS, 1), cap, I32)], axis=-1)
    offs_flat = offs.reshape(-1).astype(I32)

    xe = _gather(offs_flat, h2.reshape(t, d + LANES), slot_et.reshape(N_EXPERTS, 1, t), cap, bt2)
    ye = _ffn(xe, wts["wg"], wts["wu"], wts["wd"], _token_block(cap, 1024))
    out = _combine(offs_flat, x1.reshape(t, d), slot_te, g2, ye, n, bt2)
    return out.reshape(b, n, d)


def kernel(x_prompt, x_sample, c_prompt, c_sample, w_ada, b_ada, norm_mix, norm_ffn, w_in, q_norm, k_norm, rpb,
           ssm_a_re, ssm_a_im, ssm_log_dt, ssm_b_re, ssm_b_im, ssm_c_re, ssm_c_im, ssm_d, w_glu, w_attn_br,
           w_out, w_router, w_exp_gate, w_exp_up, w_exp_down):
    y_prompt, y_sample = x_prompt, x_sample
    for layer in range(w_ada.shape[0]):
        ssm_params = tuple(p[layer] for p in (ssm_a_re, ssm_a_im, ssm_log_dt, ssm_b_re, ssm_b_im,
                                              ssm_c_re, ssm_c_im, ssm_d))
        wts = _prep_weights(w_in[layer], q_norm[layer], k_norm[layer], rpb[layer], ssm_params, w_glu[layer],
                            w_attn_br[layer], w_out[layer], w_router[layer], w_exp_gate[layer],
                            w_exp_up[layer], w_exp_down[layer])
        y_prompt = _encoder_layer(y_prompt, c_prompt, w_ada[layer], b_ada[layer], norm_mix[layer],
                                  norm_ffn[layer], wts)
        y_sample = _encoder_layer(y_sample, c_sample, w_ada[layer], b_ada[layer], norm_mix[layer],
                                  norm_ffn[layer], wts)
    return (y_prompt, y_sample)
```

```python
import functools
import math

import jax
import jax.numpy as jnp
from jax import lax
from jax.experimental import pallas as pl
from jax.experimental.pallas import tpu as pltpu

F32 = jnp.float32
BF16 = jnp.bfloat16
I32 = jnp.int32

D_MODEL = 1024
GRID_W = 64
N_HEADS = 8
HEAD_DIM = 64
D_ATTN = N_HEADS * HEAD_DIM
WIN_R = 8
WIN_C = 16
SSM_GROUP = 16
D_SSM = 512
N_GROUPS = D_SSM // SSM_GROUP
STATE_P = 64
D_IN = 3 * D_ATTN + D_SSM + 2 * D_MODEL
N_EXPERTS = 16
EC_CAPACITY = 2
D_EXPERT = 2048
EPS = 1e-6
NEG_INF = -1e9

LANES = 128
SUBLANES = 8
BF16_ROWS = 16
VMEM_LIMIT = 56 * 1024 * 1024

SSM_L = 8
SG_GROUPS = LANES // SSM_GROUP
N_SG = N_GROUPS // SG_GROUPS
SG_STATE = SG_GROUPS * STATE_P
SCAN_ROWS = SUBLANES

ATTN_ROWS = 8
ATTN_UNROLL = 4
MERGE_SPLIT = 4
INPROJ_SPLIT = 1
SLOT_WIN = 128
GATHER_EXPERTS = 4
CUM_W = 256


def _cparams(sem):
    return pltpu.CompilerParams(dimension_semantics=sem, vmem_limit_bytes=VMEM_LIMIT)


def _split_bf16(a):
    hi = a.astype(BF16)
    lo = (a - hi.astype(F32)).astype(BF16)
    return hi, lo


def _dot(a, b):
    return jnp.dot(a, b, preferred_element_type=F32)


def _dot3(a, b):
    ah, al = _split_bf16(a)
    bh, bl = _split_bf16(b)
    return _dot(ah, bh) + (_dot(ah, bl) + _dot(al, bh))


def _sigmoid(z):
    return 1.0 / (1.0 + jnp.exp(-z))


def _ada_kernel(c_ref, w_ref, b_ref, o_ref):
    c = c_ref[...]
    s = c * _sigmoid(c)
    o_ref[...] = _dot3(s, w_ref[...]) + b_ref[...]


def _ada(c_pad, w_ada, b_ada):
    rows = c_pad.shape[0]
    n_out = w_ada.shape[1]
    return pl.pallas_call(
        _ada_kernel,
        grid=(n_out // D_MODEL,),
        in_specs=[pl.BlockSpec((rows, D_MODEL), lambda j: (0, 0)),
                  pl.BlockSpec((D_MODEL, D_MODEL), lambda j: (0, j)),
                  pl.BlockSpec((1, D_MODEL), lambda j: (0, j))],
        out_specs=pl.BlockSpec((rows, D_MODEL), lambda j: (0, j)),
        out_shape=jax.ShapeDtypeStruct((rows, n_out), F32),
        compiler_params=_cparams(("arbitrary",)),
        name="ada",
    )(c_pad, w_ada, b_ada.reshape(1, n_out))


def _inproj_kernel(x_ref, sc_ref, sh_ref, nm_ref, w_ref, qg_ref, kg_ref, ones_ref,
                   q_ref, k_ref, v_ref, u_ref, ga_ref, gs_ref, h_scr):
    def head_norm(z, gain):
        ssum = _dot((z * z).astype(BF16), ones_ref[...])
        return z * lax.rsqrt(ssum * (1.0 / HEAD_DIM) + EPS) * gain

    sub = x_ref.shape[0] // INPROJ_SPLIT
    crows = sub // SSM_L
    for kb in range(INPROJ_SPLIT):
        rows = slice(kb * sub, (kb + 1) * sub)
        x = x_ref[rows, :]
        ms = jnp.mean(x * x, axis=-1, keepdims=True)
        xn = x * lax.rsqrt(ms + EPS) * nm_ref[...]
        hf = xn * (1.0 + sc_ref[...]) + sh_ref[...]
        for c in range(D_MODEL // LANES):
            h_scr[c, rows, :] = hf[:, c * LANES:(c + 1) * LANES]
        h = hf.astype(BF16)

        def proj(lo, hi, h=h):
            return _dot(h, w_ref[:, lo:hi])

        q = head_norm(proj(0, D_ATTN), qg_ref[...]) * (HEAD_DIM ** -0.5)
        q_ref[rows, :] = q.astype(BF16)
        k = head_norm(proj(D_ATTN, 2 * D_ATTN), kg_ref[...])
        k_ref[rows, :] = k.astype(BF16)
        v_ref[rows, :] = proj(2 * D_ATTN, 3 * D_ATTN).astype(BF16)
        hp = jnp.concatenate(
            [jnp.concatenate([h_scr[c, pl.ds(kb * sub + s, crows, stride=SSM_L), :]
                              for c in range(D_MODEL // LANES)], axis=1)
             for s in range(SSM_L)], axis=0)
        u = _dot(hp.astype(BF16), w_ref[:, 3 * D_ATTN:3 * D_ATTN + D_SSM])
        for sg in range(N_SG):
            for s in range(SSM_L):
                u_ref[sg, kb * crows:(kb + 1) * crows, s * LANES:(s + 1) * LANES] = (
                    u[s * crows:(s + 1) * crows, sg * LANES:(sg + 1) * LANES].astype(BF16))
        o = 3 * D_ATTN + D_SSM
        ga_ref[rows, :] = _sigmoid(proj(o, o + D_MODEL)).astype(BF16)
        gs_ref[rows, :] = _sigmoid(proj(o + D_MODEL, o + 2 * D_MODEL)).astype(BF16)


def _inproj(x, sc1, sh1, norm_mix, w_in_bf, qg, kg, ones_bd, bt):
    b, n, d = x.shape
    tok = lambda w: pl.BlockSpec((None, bt, w), lambda i, j: (i, j, 0))
    mod = pl.BlockSpec((None, 1, d), lambda i, j: (i, 0, 0))
    full = lambda shape: pl.BlockSpec(shape, lambda i, j: tuple(0 for _ in shape))
    return pl.pallas_call(
        _inproj_kernel,
        grid=(b, n // bt),
        in_specs=[tok(d), mod, mod, full((1, d)), full((d, D_IN)),
                  full((1, D_ATTN)), full((1, D_ATTN)), full((D_ATTN, D_ATTN))],
        out_specs=[tok(D_ATTN), tok(D_ATTN), tok(D_ATTN),
                   pl.BlockSpec((None, N_SG, bt // SSM_L, SSM_L * LANES), lambda i, j: (i, 0, j, 0)),
                   tok(d), tok(d)],
        out_shape=[jax.ShapeDtypeStruct((b, n, D_ATTN), BF16)] * 3
        + [jax.ShapeDtypeStruct((b, N_SG, n // SSM_L, SSM_L * LANES), BF16)]
        + [jax.ShapeDtypeStruct((b, n, d), BF16)] * 2,
        scratch_shapes=[pltpu.VMEM((d // LANES, bt, LANES), F32)],
        compiler_params=_cparams(("parallel", "parallel")),
        name="inproj",
    )(x, sc1, sh1, norm_mix, w_in_bf, qg, kg, ones_bd)


def _attn_kernel(q_ref, kp_ref, kc_ref, kn_ref, vp_ref, vc_ref, vn_ref, bias_ref, o_ref,
                 kwin, vwin, *, rows):
    r0 = pl.program_id(1) * ATTN_ROWS
    rw = ATTN_ROWS * GRID_W
    for t, (kr, vr) in enumerate(((kp_ref, vp_ref), (kc_ref, vc_ref), (kn_ref, vn_ref))):
        kwin[t * rw:(t + 1) * rw, :] = kr[...]
        vwin[t * rw:(t + 1) * rw, :] = vr[...]
    even = lax.broadcasted_iota(I32, (GRID_W, LANES), 1) < HEAD_DIM
    nkeys = WIN_R * GRID_W
    pairs = N_HEADS // 2

    def rows_body(it, carry):
        units = []
        for sub in range(ATTN_UNROLL):
            i = it * ATTN_UNROLL + sub
            r = r0 + i
            rs = jnp.clip(r - WIN_R // 2, 0, rows - WIN_R)
            variant = r - rs
            koff = pl.multiple_of((rs - r0 + ATTN_ROWS) * GRID_W, GRID_W)
            qoff = pl.multiple_of(i * GRID_W, GRID_W)
            for hp in range(pairs):
                ls = slice(hp * LANES, (hp + 1) * LANES)
                qp = q_ref[pl.ds(qoff, GRID_W), ls]
                zero = jnp.zeros_like(qp)
                q2 = jnp.concatenate([jnp.where(even, qp, zero), jnp.where(even, zero, qp)], axis=0)
                kp = kwin[pl.ds(koff, nkeys), ls]
                s = lax.dot_general(q2, kp, (((1,), (1,)), ((), ())), preferred_element_type=F32)
                units.append((qoff, koff, ls, s + bias_ref[variant, hp]))
        probs = []
        for qoff, koff, ls, s in units:
            p = jnp.exp(s - jnp.max(s, axis=-1, keepdims=True))
            probs.append((p.astype(BF16), jnp.sum(p, axis=-1, keepdims=True)))
        for (qoff, koff, ls, _), (p, l) in zip(units, probs):
            o2 = _dot(p, vwin[pl.ds(koff, nkeys), ls]) / l
            o_ref[pl.ds(qoff, GRID_W), ls] = jnp.where(even, o2[:GRID_W], o2[GRID_W:]).astype(BF16)
        return carry

    lax.fori_loop(0, ATTN_ROWS // ATTN_UNROLL, rows_body, 0)


def _attention(q, k, v, bias_tab):
    b, n, _ = q.shape
    rows = n // GRID_W
    assert rows % ATTN_ROWS == 0 and rows >= WIN_R
    nblk = rows // ATTN_ROWS
    rw = ATTN_ROWS * GRID_W
    cur = pl.BlockSpec((None, rw, D_ATTN), lambda i, j: (i, j, 0))
    prv = pl.BlockSpec((None, rw, D_ATTN), lambda i, j: (i, jnp.maximum(j - 1, 0), 0))
    nxt = pl.BlockSpec((None, rw, D_ATTN), lambda i, j: (i, jnp.minimum(j + 1, nblk - 1), 0))
    return pl.pallas_call(
        functools.partial(_attn_kernel, rows=rows),
        grid=(b, nblk),
        in_specs=[cur, prv, cur, nxt, prv, cur, nxt,
                  pl.BlockSpec(bias_tab.shape, lambda i, j: (0, 0, 0, 0))],
        out_specs=cur,
        out_shape=jax.ShapeDtypeStruct((b, n, D_ATTN), BF16),
        scratch_shapes=[pltpu.VMEM((3 * rw, D_ATTN), BF16), pltpu.VMEM((3 * rw, D_ATTN), BF16)],
        compiler_params=_cparams(("parallel", "parallel")),
        name="attn",
    )(q, k, k, k, v, v, v, bias_tab)


def _attn_bias_table(rpb):
    var = jnp.arange(WIN_R)
    a = jnp.arange(WIN_R)
    j = jnp.arange(GRID_W)
    c_start = jnp.clip(j - WIN_C // 2, 0, GRID_W - WIN_C)
    col_ok = (j[None, :] >= c_start[:, None]) & (j[None, :] < c_start[:, None] + WIN_C)
    col_off = jnp.clip(j[None, :] - j[:, None], -(WIN_C - 1), WIN_C - 1) + WIN_C - 1
    row_off = a[None, :] - var[:, None] + WIN_R - 1
    hp = lax.Precision.HIGHEST
    row_sel = (row_off[:, :, None] == jnp.arange(2 * WIN_R - 1)).astype(F32)
    col_sel = (col_off[:, :, None] == jnp.arange(2 * WIN_C - 1)).astype(F32)
    tab = jnp.einsum('hrc,var->hvac', rpb.astype(F32), row_sel, precision=hp)
    tab = jnp.einsum('hvac,jkc->vhjak', tab, col_sel, precision=hp)
    tab = jnp.where(col_ok[None, None, :, None, :], tab, NEG_INF)
    return tab.reshape(WIN_R, N_HEADS // 2, 2 * GRID_W, WIN_R * GRID_W)


def _ssm_tables(a_re, a_im, log_dt, b_re, b_im, c_re, c_im, d_skip):
    L = SSM_L
    hp = lax.Precision.HIGHEST
    lam_re, lam_im = a_re.astype(F32), a_im.astype(F32)
    dt = jnp.exp(log_dt.astype(F32))[..., None]
    ldt_re, ldt_im = lam_re * dt, lam_im * dt

    def apow(kk):
        mag = jnp.exp(ldt_re * kk)
        return mag * jnp.cos(ldt_im * kk), mag * jnp.sin(ldt_im * kk)

    a1_re, a1_im = apow(1.0)
    den = lam_re * lam_re + lam_im * lam_im
    co_re = ((a1_re - 1.0) * lam_re + a1_im * lam_im) / den
    co_im = (a1_im * lam_re - (a1_re - 1.0) * lam_im) / den
    bm_re, bm_im = b_re.astype(F32), b_im.astype(F32)
    bb_re = co_re[..., None] * bm_re - co_im[..., None] * bm_im
    bb_im = co_re[..., None] * bm_im + co_im[..., None] * bm_re
    cm_re, cm_im = c_re.astype(F32), c_im.astype(F32)

    lags = jnp.arange(L, dtype=F32)
    pw_re, pw_im = apow(lags[:, None, None, None])
    e_re = cm_re[None] * pw_re[:, :, :, None, :] - cm_im[None] * pw_im[:, :, :, None, :]
    e_im = cm_re[None] * pw_im[:, :, :, None, :] + cm_im[None] * pw_re[:, :, :, None, :]
    kern = (jnp.einsum('ldgxp,dgpc->ldgxc', e_re, bb_re, precision=hp)
            - jnp.einsum('ldgxp,dgpc->ldgxc', e_im, bb_im, precision=hp))
    s_idx = jnp.arange(L)[:, None]
    t_idx = jnp.arange(L)[None, :]
    lag = jnp.arange(L)
    sel_f = ((t_idx - s_idx)[:, :, None] == lag).astype(F32)
    sel_b = ((s_idx - t_idx)[:, :, None] == lag).astype(F32)
    m = (jnp.einsum('stl,lgxc->gsctx', sel_f, kern[:, 0], precision=hp)
         + jnp.einsum('stl,lgxc->gsctx', sel_b, kern[:, 1], precision=hp))

    def block_diag_cols(a, g_axis):
        w = a.shape[-1]
        col_group = jnp.arange(SG_GROUPS * w) // w
        shape = [1] * a.ndim
        shape[g_axis] = SG_GROUPS
        own = jnp.arange(SG_GROUPS).reshape(shape) == col_group
        return jnp.where(own, jnp.tile(a, SG_GROUPS), 0.0)

    m = m.reshape(N_SG, SG_GROUPS, L, SSM_GROUP, L, SSM_GROUP).transpose(0, 2, 1, 3, 4, 5)
    t_mat = jnp.concatenate(
        [block_diag_cols(m[:, :, :, :, t, :], 2).reshape(N_SG, L * LANES, LANES) for t in range(L)], axis=-1)

    tau = jnp.arange(L, dtype=F32)[:, None, None]
    parts = []
    for d, expo in ((0, L - 1 - tau), (1, tau)):
        p_re = jnp.exp(ldt_re[d][None] * expo) * jnp.cos(ldt_im[d][None] * expo)
        p_im = jnp.exp(ldt_re[d][None] * expo) * jnp.sin(ldt_im[d][None] * expo)
        br, bi = bb_re[d].transpose(0, 2, 1), bb_im[d].transpose(0, 2, 1)
        parts.append(p_re[:, :, None, :] * br[None] - p_im[:, :, None, :] * bi[None])
        parts.append(p_re[:, :, None, :] * bi[None] + p_im[:, :, None, :] * br[None])
    wb_mat = jnp.concatenate(
        [block_diag_cols(part.reshape(L, N_SG, SG_GROUPS, SSM_GROUP, STATE_P).transpose(1, 0, 2, 3, 4), 2)
         .reshape(N_SG, L * LANES, SG_STATE) for part in parts], axis=-1)

    parts = []
    for d, expo in ((0, tau + 1.0), (1, L - tau)):
        p_re = jnp.exp(ldt_re[d][None] * expo) * jnp.cos(ldt_im[d][None] * expo)
        p_im = jnp.exp(ldt_re[d][None] * expo) * jnp.sin(ldt_im[d][None] * expo)
        cr, ci = cm_re[d], cm_im[d]
        z_re = cr[None] * p_re[:, :, None, :] - ci[None] * p_im[:, :, None, :]
        z_im = cr[None] * p_im[:, :, None, :] + ci[None] * p_re[:, :, None, :]
        parts.append(z_re.transpose(1, 3, 0, 2))
        parts.append(-z_im.transpose(1, 3, 0, 2))
    wc = jnp.stack(parts, axis=0).reshape(4, N_SG, SG_GROUPS, STATE_P, L, SSM_GROUP).transpose(1, 0, 2, 3, 4, 5)
    wc_mat = jnp.concatenate(
        [block_diag_cols(wc[:, :, :, :, t, :], 2).reshape(N_SG, 4 * SG_STATE, LANES) for t in range(L)], axis=-1)

    row = jnp.arange(SCAN_ROWS)
    tiles = []
    for d in range(2):
        kinds = []
        for sh in (1, 2, 4):
            keep = (row >= sh) if d == 0 else (row <= SCAN_ROWS - 1 - sh)
            kinds.append((jnp.full((SCAN_ROWS,), float(L * sh), F32), keep))
        expo = (row + 1.0) if d == 0 else (SCAN_ROWS - row).astype(F32)
        kinds.append((L * expo.astype(F32), jnp.ones((SCAN_ROWS,), bool)))
        per_kind = []
        for expo_r, keep in kinds:
            e = expo_r[:, None, None]
            p_re = jnp.exp(ldt_re[d][None] * e) * jnp.cos(ldt_im[d][None] * e)
            p_im = jnp.exp(ldt_re[d][None] * e) * jnp.sin(ldt_im[d][None] * e)
            k3 = keep[:, None, None]
            per_kind.append(jnp.stack([jnp.where(k3, p_re, 0.0), jnp.where(k3, p_im, 0.0)], axis=0))
        tiles.append(jnp.stack(per_kind, axis=0))
    cst = jnp.stack(tiles, axis=0).reshape(2, 4, 2, SCAN_ROWS, N_SG, SG_STATE)
    cst = cst.transpose(4, 0, 1, 2, 3, 5)

    dsk = jnp.tile(d_skip.astype(F32).reshape(N_SG, 1, LANES), (1, 1, L))
    return t_mat.astype(BF16), wb_mat.astype(BF16), wc_mat.astype(BF16), cst, dsk


def _ssm_kernel(u_ref, t_ref, wb_ref, wc_ref, cst_ref, d_ref, y_ref, stf_ref, stb_ref, *, n_chunks, mm_rows):
    n_mm = n_chunks // mm_rows
    tiles = mm_rows // SCAN_ROWS
    half = SG_STATE
    row_id = lax.broadcasted_iota(I32, (SCAN_ROWS, half), 0)
    zero = jnp.zeros((1, half), F32)

    def block(c):
        return pl.ds(pl.multiple_of(c * mm_rows, mm_rows), mm_rows)

    def scan_tile(st_ref, d, row0, carry):
        rows = pl.ds(pl.multiple_of(row0, SCAN_ROWS), SCAN_ROWS)
        xr = st_ref[rows, 0:half]
        xi = st_ref[rows, half:2 * half]
        for si, sh in enumerate((1, 2, 4)):
            ar, ai = cst_ref[d, si, 0], cst_ref[d, si, 1]
            shift = sh if d == 0 else SCAN_ROWS - sh
            pr, pi = pltpu.roll(xr, shift, 0), pltpu.roll(xi, shift, 0)
            xr, xi = xr + (ar * pr - ai * pi), xi + (ar * pi + ai * pr)
        cr, ci = carry
        ar, ai = cst_ref[d, 3, 0], cst_ref[d, 3, 1]
        xr, xi = xr + (ar * cr - ai * ci), xi + (ar * ci + ai * cr)
        if d == 0:
            edge, shift, last = 0, 1, SCAN_ROWS - 1
        else:
            edge, shift, last = SCAN_ROWS - 1, SCAN_ROWS - 1, 0
        st_ref[rows, 0:half] = jnp.where(row_id == edge, cr, pltpu.roll(xr, shift, 0))
        st_ref[rows, half:2 * half] = jnp.where(row_id == edge, ci, pltpu.roll(xi, shift, 0))
        return xr[last:last + 1, :], xi[last:last + 1, :]

    def inject_fwd(c, carry):
        stf_ref[block(c), :] = _dot(u_ref[block(c), :], wb_ref[:, 0:2 * half])
        return carry

    lax.fori_loop(0, n_mm, inject_fwd, 0)

    def forward(c, carry):
        u = u_ref[block(c), :]
        y_ref[block(c), :] = _dot(u, t_ref[...]) + u.astype(F32) * d_ref[...]
        stb_ref[block(c), :] = _dot(u, wb_ref[:, 2 * half:4 * half])
        for k in range(tiles):
            carry = scan_tile(stf_ref, 0, c * mm_rows + k * SCAN_ROWS, carry)
        return carry

    lax.fori_loop(0, n_mm, forward, (zero, zero))

    def backward(i, carry):
        c = n_mm - 1 - i
        y_ref[block(c), :] += _dot(stf_ref[block(c), :].astype(BF16), wc_ref[0:2 * half, :])
        for k in reversed(range(tiles)):
            carry = scan_tile(stb_ref, 1, c * mm_rows + k * SCAN_ROWS, carry)
        return carry

    lax.fori_loop(0, n_mm, backward, (zero, zero))

    def eject_bwd(c, carry):
        y_ref[block(c), :] += _dot(stb_ref[block(c), :].astype(BF16), wc_ref[2 * half:4 * half, :])
        return carry

    lax.fori_loop(0, n_mm, eject_bwd, 0)


def _ssm(u, tabs):
    t_mat, wb_mat, wc_mat, cst, dsk = tabs
    b, _, n_chunks, lw = u.shape
    mm_rows = min(256, n_chunks)
    assert n_chunks % mm_rows == 0 and n_chunks % SCAN_ROWS == 0
    one = pl.Buffered(1)
    wspec = lambda shape: pl.BlockSpec((None,) + shape, lambda q, i: (q,) + tuple(0 for _ in shape),
                                       pipeline_mode=one)
    return pl.pallas_call(
        functools.partial(_ssm_kernel, n_chunks=n_chunks, mm_rows=mm_rows),
        grid=(N_SG, b),
        in_specs=[pl.BlockSpec((None, None, n_chunks, lw), lambda q, i: (i, q, 0, 0)),
                  wspec((lw, lw)), wspec((lw, 4 * SG_STATE)), wspec((4 * SG_STATE, lw)),
                  wspec((2, 4, 2, SCAN_ROWS, SG_STATE)), wspec((1, lw))],
        out_specs=pl.BlockSpec((None, None, n_chunks, lw), lambda q, i: (i, q, 0, 0)),
        out_shape=jax.ShapeDtypeStruct((b, N_SG, n_chunks, lw), F32),
        scratch_shapes=[pltpu.VMEM((n_chunks, 2 * SG_STATE), F32)] * 2,
        compiler_params=_cparams(("arbitrary", "arbitrary")),
        name="ssm",
    )(u, t_mat, wb_mat, wc_mat, cst, dsk)


def _merge_kernel(x_ref, attn_ref, y_ref, ga_ref, gs_ref, g1_ref, sc_ref, sh_ref, nf_ref,
                  wab_ref, wglu_ref, wout_ref, wr_ref, x1_ref, h2_ref, lg_ref, gel_scr):
    sub = x_ref.shape[0] // MERGE_SPLIT
    crows = sub // SSM_L
    blocks = [(k, slice(k * sub, (k + 1) * sub)) for k in range(MERGE_SPLIT)]

    ab = [_dot(attn_ref[rows, :], wab_ref[...]) for _, rows in blocks]

    gel = []
    for k, rows in blocks:
        for sg in range(N_SG):
            for s in range(SSM_L):
                y = y_ref[sg, k * crows:(k + 1) * crows, s * LANES:(s + 1) * LANES]
                gel_scr[sg, k * sub + s * crows:k * sub + (s + 1) * crows, :] = (
                    0.5 * y * (1.0 + jnp.tanh(math.sqrt(2.0 / math.pi) * (y + 0.044715 * (y * y * y)))))
        gel.append(jnp.concatenate(
            [jnp.concatenate([gel_scr[sg, pl.ds(k * sub + j, SSM_L, stride=crows), :] for j in range(crows)], axis=0)
             for sg in range(N_SG)], axis=1).astype(BF16))
    glu = [_dot(g, wglu_ref[...]) for g in gel]
    merged = []
    for (k, rows), ab_k, glu_k in zip(blocks, ab, glu):
        sb = glu_k[:, :D_MODEL] * _sigmoid(glu_k[:, D_MODEL:])
        merged.append((ga_ref[rows, :].astype(F32) * ab_k + gs_ref[rows, :].astype(F32) * sb).astype(BF16))
    mixed = [_dot(m, wout_ref[...]) for m in merged]
    h2s = []
    for (k, rows), mix in zip(blocks, mixed):
        x1 = x_ref[rows, :] + g1_ref[...] * mix
        x1_ref[rows, :] = x1
        ms = jnp.mean(x1 * x1, axis=-1, keepdims=True)
        h2 = x1 * lax.rsqrt(ms + EPS) * nf_ref[...] * (1.0 + sc_ref[...]) + sh_ref[...]
        h2_ref[rows, :D_MODEL] = h2.astype(BF16)
        h2s.append(h2)
    for (k, rows), h2 in zip(blocks, h2s):
        lg = _dot3(h2, wr_ref[...])
        lg_ref[rows, :] = lg
        valid = lax.broadcasted_iota(I32, lg.shape, 1) < N_EXPERTS
        m = jnp.max(jnp.where(valid, lg, -jnp.inf), axis=-1, keepdims=True)
        ex = jnp.where(valid, jnp.exp(lg - m), 0.0)
        aff = ex / jnp.sum(ex, axis=-1, keepdims=True)
        hi = aff.astype(BF16).astype(F32)
        mid = (aff - hi).astype(BF16).astype(F32)
        lo = (aff - hi - mid).astype(BF16).astype(F32)
        parts = hi + pltpu.roll(mid, N_EXPERTS, 1) + pltpu.roll(lo, 2 * N_EXPERTS, 1)
        h2_ref[rows, D_MODEL:] = parts.astype(BF16)


def _merge(x, attn, y4, ga, gs, g1, sc2, sh2, norm_ffn, wab, wglu, wout, wr_pad, bt):
    b, n, d = x.shape
    tok = lambda w: pl.BlockSpec((None, bt, w), lambda i, j: (i, j, 0))
    mod = pl.BlockSpec((None, 1, d), lambda i, j: (i, 0, 0))
    full = lambda shape: pl.BlockSpec(shape, lambda i, j: tuple(0 for _ in shape))
    return pl.pallas_call(
        _merge_kernel,
        grid=(b, n // bt),
        in_specs=[tok(d), tok(D_ATTN),
                  pl.BlockSpec((None, N_SG, bt // SSM_L, SSM_L * LANES), lambda i, j: (i, 0, j, 0)),
                  tok(d), tok(d), mod, mod, mod, full((1, d)),
                  full((D_ATTN, d)), full((D_SSM, 2 * d)), full((d, d)), full((d, LANES))],
        out_specs=[tok(d), tok(d + LANES), tok(LANES)],
        out_shape=[jax.ShapeDtypeStruct((b, n, d), F32), jax.ShapeDtypeStruct((b, n, d + LANES), BF16),
                   jax.ShapeDtypeStruct((b, n, LANES), F32)],
        scratch_shapes=[pltpu.VMEM((N_SG, bt, LANES), F32)],
        compiler_params=_cparams(("parallel", "parallel")),
        name="merge",
    )(x, attn, y4, ga, gs, g1, sc2, sh2, norm_ffn, wab, wglu, wout, wr_pad)


def _route_kernel(lg_ref, tri_ref, slot_ref, cnt_ref, aff_ref, *, cap):
    lg = lg_ref[...]
    m = jnp.max(lg, axis=1, keepdims=True)
    e = jnp.exp(lg - m)
    aff = e / jnp.sum(e, axis=1, keepdims=True)
    aff_ref[...] = aff

    def count(mask):
        c = jnp.sum(jnp.where(mask, 1.0, 0.0), axis=0, keepdims=True)
        return jnp.sum(c, axis=2, keepdims=True)

    def as_float(bits):
        return lax.bitcast_convert_type(bits, F32)

    def bit_step(i, thr):
        cand = thr | (jnp.int32(1) << (30 - i))
        return jnp.where(count(aff >= as_float(cand)) >= cap, cand, thr)

    thr3 = as_float(lax.fori_loop(0, 31, bit_step, jnp.zeros((1, N_EXPERTS, 1), I32)))
    need = (cap - count(aff > thr3))[0]
    thr = thr3[0]
    n_chunks = lg.shape[0]
    carry0 = jnp.zeros((N_EXPERTS, 1), F32)

    def prefix(flag, carry):
        f = jnp.where(flag, 1.0, 0.0)
        inc = _dot(f.astype(BF16), tri_ref[...]) + carry
        return inc, inc - f

    def tie_body(c, carry):
        a = aff_ref[c]
        tie = a == thr
        inc, rank = prefix(tie, carry)
        sel = (a > thr) | (tie & (rank < need))
        slot_ref[c] = jnp.where(sel, 1, 0).astype(I32)
        return inc[:, CUM_W - 1:CUM_W]

    lax.fori_loop(0, n_chunks, tie_body, carry0)

    def slot_body(c, carry):
        sel = slot_ref[c] > 0
        inc, excl = prefix(sel, carry)
        slot_ref[c] = jnp.where(sel, excl, -1.0).astype(I32)
        cnt_ref[c] = excl.astype(I32)
        return inc[:, CUM_W - 1:CUM_W]

    lax.fori_loop(0, n_chunks, slot_body, carry0)


def _route(logits_c, tri, cap):
    nc = logits_c.shape[0]
    shp = (nc, N_EXPERTS, CUM_W)
    return pl.pallas_call(
        functools.partial(_route_kernel, cap=cap),
        grid=(1,),
        in_specs=[pl.BlockSpec(shp, lambda i: (0, 0, 0)), pl.BlockSpec((CUM_W, CUM_W), lambda i: (0, 0))],
        out_specs=[pl.BlockSpec(shp, lambda i: (0, 0, 0))] * 2,
        out_shape=[jax.ShapeDtypeStruct(shp, I32)] * 2,
        scratch_shapes=[pltpu.VMEM(shp, F32)],
        compiler_params=_cparams(("arbitrary",)),
        name="route",
    )(logits_c, tri)


def _gather_kernel(offs_ref, h_ref, slot_ref, xe_ref, *, nb, cap, win):
    eg = pl.program_id(0)
    blk = pl.program_id(1)

    @pl.when(blk == 0)
    def _():
        xe_ref[...] = jnp.zeros_like(xe_ref)

    riota = lax.broadcasted_iota(I32, (win, 1), 0)

    def window(ee, w):
        off = offs_ref[(eg * GATHER_EXPERTS + ee) * (nb + 1) + blk]
        nominal = (off // BF16_ROWS) * BF16_ROWS + w * win
        start = pl.multiple_of(jnp.minimum(nominal, cap - win), BF16_ROWS)
        slot = slot_ref[ee]
        hit = (slot == riota + start) & (slot >= nominal)
        return jnp.where(hit, 1.0, 0.0).astype(BF16), start

    firsts = [window(ee, 0) for ee in range(GATHER_EXPERTS)]
    picked = _dot(jnp.concatenate([oh for oh, _ in firsts], axis=0), h_ref[...]).astype(BF16)
    for ee, (_, start) in enumerate(firsts):
        xe_ref[ee, pl.ds(start, win), :] += picked[ee * win:(ee + 1) * win]

    for ee in range(GATHER_EXPERTS):
        off = offs_ref[(eg * GATHER_EXPERTS + ee) * (nb + 1) + blk]
        end = offs_ref[(eg * GATHER_EXPERTS + ee) * (nb + 1) + blk + 1]
        n_win = (end - (off // BF16_ROWS) * BF16_ROWS + win - 1) // win

        def extra(w, carry, ee=ee):
            onehot, start = window(ee, w)
            xe_ref[ee, pl.ds(start, win), :] += _dot(onehot, h_ref[...]).astype(BF16)
            return carry

        lax.fori_loop(1, n_win, extra, 0)


def _gather(offs_flat, h2, slot3, cap, bt):
    t, d = h2.shape
    nb = t // bt
    win = min(SLOT_WIN, cap)
    ge = GATHER_EXPERTS
    gs = pltpu.PrefetchScalarGridSpec(
        num_scalar_prefetch=1,
        grid=(N_EXPERTS // ge, nb),
        in_specs=[pl.BlockSpec((bt, d), lambda e, j, offs: (j, 0)),
                  pl.BlockSpec((ge, 1, bt), lambda e, j, offs: (e, 0, j))],
        out_specs=pl.BlockSpec((ge, cap, d), lambda e, j, offs: (e, 0, 0), pipeline_mode=pl.Buffered(1)),
    )
    return pl.pallas_call(
        functools.partial(_gather_kernel, nb=nb, cap=cap, win=win),
        grid_spec=gs,
        out_shape=jax.ShapeDtypeStruct((N_EXPERTS, cap, d), BF16),
        compiler_params=_cparams(("arbitrary", "arbitrary")),
        name="gather",
    )(offs_flat, h2, slot3)


def _ffn_kernel(x_ref, wg_ref, wu_ref, wd_ref, y_ref, *, fchunk):
    x = x_ref[:, :D_MODEL]
    acc = jnp.zeros((x.shape[0], D_MODEL), F32)
    for f in range(D_EXPERT // fchunk):
        fs = slice(f * fchunk, (f + 1) * fchunk)
        a = _dot(x, wg_ref[:, fs])
        u = _dot(x, wu_ref[:, fs])
        hmid = (a * _sigmoid(a) * u).astype(BF16)
        acc = acc + _dot(hmid, wd_ref[fs, :])
    parts = x_ref[:, D_MODEL:].astype(F32)
    lane = lax.broadcasted_iota(I32, parts.shape, 1)
    mine = (lane % N_EXPERTS == pl.program_id(0)) & (lane < 3 * N_EXPERTS)
    gate = jnp.sum(jnp.where(mine, parts, 0.0), axis=-1, keepdims=True)
    y_ref[...] = (gate * acc).astype(BF16)


def _ffn(xe, wg, wu, wd, tm):
    e, cap, dx = xe.shape
    d = D_MODEL
    return pl.pallas_call(
        functools.partial(_ffn_kernel, fchunk=512),
        grid=(e, cap // tm),
        in_specs=[pl.BlockSpec((None, tm, dx), lambda i, j: (i, j, 0)),
                  pl.BlockSpec((None, d, D_EXPERT), lambda i, j: (i, 0, 0)),
                  pl.BlockSpec((None, d, D_EXPERT), lambda i, j: (i, 0, 0)),
                  pl.BlockSpec((None, D_EXPERT, d), lambda i, j: (i, 0, 0))],
        out_specs=pl.BlockSpec((None, tm, d), lambda i, j: (i, j, 0)),
        out_shape=jax.ShapeDtypeStruct((e, cap, d), BF16),
        compiler_params=_cparams(("parallel", "parallel")),
        name="ffn",
    )(xe, wg, wu, wd)


def _window_copy(ye_hbm, buf, sem, e, start, win):
    return pltpu.make_async_copy(ye_hbm.at[e, pl.ds(start, win), :], buf, sem)


def _combine_kernel(offs_ref, x1_ref, slot_ref, g2_ref, ye_hbm, o_ref, ybuf, xbuf, lhs, sems, xsem,
                    *, nb, cap, win):
    blk = pl.program_id(0)

    def first_window(b_, e):
        off = offs_ref[e * (nb + 1) + b_]
        a0 = (off // BF16_ROWS) * BF16_ROWS
        return a0, pl.multiple_of(jnp.minimum(a0, cap - win), BF16_ROWS)

    def window_copies(b_, half):
        return [_window_copy(ye_hbm, ybuf.at[half, pl.ds(e * win, win)], sems.at[half, e], e,
                             first_window(b_, e)[1], win) for e in range(N_EXPERTS)]

    @pl.when(blk == 0)
    def _():
        for cp in window_copies(0, 0):
            cp.start()

    @pl.when(blk + 1 < nb)
    def _():
        for cp in window_copies(blk + 1, (blk + 1) % 2):
            cp.start()

    starts = [first_window(blk, e) for e in range(N_EXPERTS)]
    liota = lax.broadcasted_iota(I32, (1, win), 1)
    slots = slot_ref[...]
    for e in range(N_EXPERTS):
        hit = slots[:, e:e + 1] == liota + starts[e][1]
        lhs[:, e * win:(e + 1) * win] = jnp.where(hit, 1.0, 0.0).astype(BF16)
    for cp in window_copies(blk, blk % 2):
        cp.wait()
    o_ref[...] = x1_ref[...] + g2_ref[...] * _dot(lhs[...], ybuf[blk % 2])

    for e in range(N_EXPERTS):
        a0 = starts[e][0]
        end = offs_ref[e * (nb + 1) + blk + 1]
        n_win = (end - a0 + win - 1) // win

        def extra(w, carry, e=e, a0=a0):
            nominal = a0 + w * win
            st = pl.multiple_of(jnp.minimum(nominal, cap - win), BF16_ROWS)
            cp = _window_copy(ye_hbm, xbuf, xsem, e, st, win)
            cp.start()
            cp.wait()
            scol = slot_ref[:, e:e + 1]
            hit = (scol == liota + st) & (scol >= nominal)
            o_ref[...] += g2_ref[...] * _dot(jnp.where(hit, 1.0, 0.0).astype(BF16), xbuf[...])
            return carry

        lax.fori_loop(1, n_win, extra, 0)


def _combine(offs_flat, x1, slot_t, g2, ye, n_per_batch, bt):
    t, d = x1.shape
    nb = t // bt
    cap = ye.shape[1]
    win = min(SLOT_WIN, cap)
    per = n_per_batch // bt
    gs = pltpu.PrefetchScalarGridSpec(
        num_scalar_prefetch=1,
        grid=(nb,),
        in_specs=[pl.BlockSpec((bt, d), lambda j, offs: (j, 0)),
                  pl.BlockSpec((bt, N_EXPERTS), lambda j, offs: (j, 0)),
                  pl.BlockSpec((None, 1, d), lambda j, offs: (j // per, 0, 0)),
                  pl.BlockSpec(memory_space=pl.ANY)],
        out_specs=pl.BlockSpec((bt, d), lambda j, offs: (j, 0)),
        scratch_shapes=[pltpu.VMEM((2, N_EXPERTS * win, d), BF16), pltpu.VMEM((win, d), BF16),
                        pltpu.VMEM((bt, N_EXPERTS * win), BF16),
                        pltpu.SemaphoreType.DMA((2, N_EXPERTS)), pltpu.SemaphoreType.DMA(())],
    )
    return pl.pallas_call(
        functools.partial(_combine_kernel, nb=nb, cap=cap, win=win),
        grid_spec=gs,
        out_shape=jax.ShapeDtypeStruct((t, d), F32),
        compiler_params=_cparams(("arbitrary",)),
        name="combine",
    )(offs_flat, x1, slot_t, g2, ye)


def _prep_weights(w_in, q_norm, k_norm, rpb, ssm_params, w_glu, w_attn_br, w_out, w_router,
                  w_exp_gate, w_exp_up, w_exp_down):
    head = jnp.arange(D_ATTN) // HEAD_DIM
    return dict(
        w_in=w_in.astype(BF16),
        qg=jnp.tile(q_norm.astype(F32), N_HEADS).reshape(1, D_ATTN),
        kg=jnp.tile(k_norm.astype(F32), N_HEADS).reshape(1, D_ATTN),
        ones_bd=(head[:, None] == head[None, :]).astype(BF16),
        bias_tab=_attn_bias_table(rpb),
        ssm=_ssm_tables(*ssm_params),
        wglu=w_glu.astype(BF16), wab=w_attn_br.astype(BF16), wout=w_out.astype(BF16),
        wr=jnp.pad(w_router.astype(F32), ((0, 0), (0, LANES - N_EXPERTS))),
        wg=w_exp_gate.astype(BF16), wu=w_exp_up.astype(BF16), wd=w_exp_down.astype(BF16),
        tri=(jnp.arange(CUM_W)[:, None] <= jnp.arange(CUM_W)[None, :]).astype(BF16),
    )


def _token_block(n, want):
    bt = min(want, n)
    assert n % bt == 0
    return bt


def _encoder_layer(x, c, w_ada, b_ada, norm_mix, norm_ffn, wts):
    b, n, d = x.shape
    t = b * n
    cap = EC_CAPACITY * t // N_EXPERTS

    c_pad = jnp.pad(c.astype(F32), ((0, (-b) % SUBLANES), (0, 0)))
    mod = _ada(c_pad, w_ada, b_ada)[:b]
    sh1, sc1, g1, sh2, sc2, g2 = [m.reshape(b, 1, d) for m in jnp.split(mod, 6, axis=-1)]

    bt = _token_block(n, 512)
    q, k, v, u4, ga, gs = _inproj(x, sc1, sh1, norm_mix.reshape(1, d), wts["w_in"], wts["qg"], wts["kg"],
                                  wts["ones_bd"], bt)
    attn = _attention(q, k, v, wts["bias_tab"])
    y4 = _ssm(u4, wts["ssm"])
    x1, h2, logits = _merge(x, attn, y4, ga, gs, g1, sc2, sh2, norm_ffn.reshape(1, d),
                            wts["wab"], wts["wglu"], wts["wout"], wts["wr"], bt)

    lg = logits.reshape(t, LANES)[:, :N_EXPERTS]
    lg_c = lg.reshape(t // CUM_W, CUM_W, N_EXPERTS).transpose(0, 2, 1)
    slot_c, cnt_c = _route(lg_c, wts["tri"], cap)
    slot_et = slot_c.transpose(1, 0, 2).reshape(N_EXPERTS, t)
    slot_te = slot_et.T

    bt2 = _token_block(t, 512)
    cnt_at_block = cnt_c.transpose(1, 0, 2).reshape(N_EXPERTS, t)[:, ::bt2]
    offs = jnp.concatenate([cnt_at_block, jnp.full((N_EXPERTS, 1), cap, I32)], axis=-1)
    offs_flat = offs.reshape(-1).astype(I32)

    xe = _gather(offs_flat, h2.reshape(t, d + LANES), slot_et.reshape(N_EXPERTS, 1, t), cap, bt2)
    ye = _ffn(xe, wts["wg"], wts["wu"], wts["wd"], _token_block(cap, 1024))
    out = _combine(offs_flat, x1.reshape(t, d), slot_te, g2, ye, n, bt2)
    return out.reshape(b, n, d)


def kernel(x_prompt, x_sample, c_prompt, c_sample, w_ada, b_ada, norm_mix, norm_ffn, w_in, q_norm, k_norm, rpb,
           ssm_a_re, ssm_a_im, ssm_log_dt, ssm_b_re, ssm_b_im, ssm_c_re, ssm_c_im, ssm_d, w_glu, w_attn_br,
           w_out, w_router, w_exp_gate, w_exp_up, w_exp_down):
    y_prompt, y_sample = x_prompt, x_sample
    for layer in range(w_ada.shape[0]):
        ssm_params = tuple(p[layer] for p in (ssm_a_re, ssm_a_im, ssm_log_dt, ssm_b_re, ssm_b_im,
                                              ssm_c_re, ssm_c_im, ssm_d))
        wts = _prep_weights(w_in[layer], q_norm[layer], k_norm[layer], rpb[layer], ssm_params, w_glu[layer],
                            w_attn_br[layer], w_out[layer], w_router[layer], w_exp_gate[layer],
                            w_exp_up[layer], w_exp_down[layer])
        y_prompt = _encoder_layer(y_prompt, c_prompt, w_ada[layer], b_ada[layer], norm_mix[layer],
                                  norm_ffn[layer], wts)
        y_sample = _encoder_layer(y_sample, c_sample, w_ada[layer], b_ada[layer], norm_mix[layer],
                                  norm_ffn[layer], wts)
    return (y_prompt, y_sample)
```

```python
import functools
import math

import jax
import jax.numpy as jnp
from jax import lax
from jax.experimental import pallas as pl
from jax.experimental.pallas import tpu as pltpu

F32 = jnp.float32
BF16 = jnp.bfloat16
I32 = jnp.int32

D_MODEL = 1024
GRID_W = 64
N_HEADS = 8
HEAD_DIM = 64
D_ATTN = N_HEADS * HEAD_DIM
WIN_R = 8
WIN_C = 16
SSM_GROUP = 16
D_SSM = 512
N_GROUPS = D_SSM // SSM_GROUP
STATE_P = 64
D_IN = 3 * D_ATTN + D_SSM + 2 * D_MODEL
N_EXPERTS = 16
EC_CAPACITY = 2
D_EXPERT = 2048
EPS = 1e-6
NEG_INF = -1e9

LANES = 128
SUBLANES = 8
BF16_ROWS = 16
VMEM_LIMIT = 56 * 1024 * 1024

SSM_L = 8
SG_GROUPS = LANES // SSM_GROUP
N_SG = N_GROUPS // SG_GROUPS
SG_STATE = SG_GROUPS * STATE_P
SCAN_ROWS = SUBLANES

ATTN_ROWS = 8
ATTN_UNROLL = 4
MERGE_SPLIT = 4
INPROJ_SPLIT = 1
SLOT_WIN = 128
GATHER_WIN = 96
GATHER_EXPERTS = 4
CUM_W = 256


def _cparams(sem):
    return pltpu.CompilerParams(dimension_semantics=sem, vmem_limit_bytes=VMEM_LIMIT)


def _split_bf16(a):
    hi = a.astype(BF16)
    lo = (a - hi.astype(F32)).astype(BF16)
    return hi, lo


def _dot(a, b):
    return jnp.dot(a, b, preferred_element_type=F32)


def _dot3(a, b):
    ah, al = _split_bf16(a)
    bh, bl = _split_bf16(b)
    return _dot(ah, bh) + (_dot(ah, bl) + _dot(al, bh))


def _sigmoid(z):
    return 1.0 / (1.0 + jnp.exp(-z))


def _ada_kernel(c_ref, w_ref, b_ref, o_ref):
    c = c_ref[...]
    s = c * _sigmoid(c)
    o_ref[...] = _dot3(s, w_ref[...]) + b_ref[...]


def _ada(c_pad, w_ada, b_ada):
    rows = c_pad.shape[0]
    n_out = w_ada.shape[1]
    return pl.pallas_call(
        _ada_kernel,
        grid=(n_out // D_MODEL,),
        in_specs=[pl.BlockSpec((rows, D_MODEL), lambda j: (0, 0)),
                  pl.BlockSpec((D_MODEL, D_MODEL), lambda j: (0, j)),
                  pl.BlockSpec((1, D_MODEL), lambda j: (0, j))],
        out_specs=pl.BlockSpec((rows, D_MODEL), lambda j: (0, j)),
        out_shape=jax.ShapeDtypeStruct((rows, n_out), F32),
        compiler_params=_cparams(("arbitrary",)),
        name="ada",
    )(c_pad, w_ada, b_ada.reshape(1, n_out))


def _inproj_kernel(x_ref, sc_ref, sh_ref, nm_ref, w_ref, qg_ref, kg_ref, ones_ref,
                   q_ref, k_ref, v_ref, u_ref, ga_ref, gs_ref, h_scr):
    def head_norm(z, gain):
        ssum = _dot((z * z).astype(BF16), ones_ref[...])
        return z * lax.rsqrt(ssum * (1.0 / HEAD_DIM) + EPS) * gain

    sub = x_ref.shape[0] // INPROJ_SPLIT
    crows = sub // SSM_L
    for kb in range(INPROJ_SPLIT):
        rows = slice(kb * sub, (kb + 1) * sub)
        x = x_ref[rows, :]
        ms = jnp.mean(x * x, axis=-1, keepdims=True)
        xn = x * lax.rsqrt(ms + EPS) * nm_ref[...]
        hf = xn * (1.0 + sc_ref[...]) + sh_ref[...]
        for c in range(D_MODEL // LANES):
            h_scr[c, rows, :] = hf[:, c * LANES:(c + 1) * LANES]
        h = hf.astype(BF16)

        def proj(lo, hi, h=h):
            return _dot(h, w_ref[:, lo:hi])

        q = head_norm(proj(0, D_ATTN), qg_ref[...]) * (HEAD_DIM ** -0.5)
        q_ref[rows, :] = q.astype(BF16)
        k = head_norm(proj(D_ATTN, 2 * D_ATTN), kg_ref[...])
        k_ref[rows, :] = k.astype(BF16)
        v_ref[rows, :] = proj(2 * D_ATTN, 3 * D_ATTN).astype(BF16)
        hp = jnp.concatenate(
            [jnp.concatenate([h_scr[c, pl.ds(kb * sub + s, crows, stride=SSM_L), :]
                              for c in range(D_MODEL // LANES)], axis=1)
             for s in range(SSM_L)], axis=0)
        u = _dot(hp.astype(BF16), w_ref[:, 3 * D_ATTN:3 * D_ATTN + D_SSM])
        for sg in range(N_SG):
            for s in range(SSM_L):
                u_ref[sg, kb * crows:(kb + 1) * crows, s * LANES:(s + 1) * LANES] = (
                    u[s * crows:(s + 1) * crows, sg * LANES:(sg + 1) * LANES].astype(BF16))
        o = 3 * D_ATTN + D_SSM
        ga_ref[rows, :] = _sigmoid(proj(o, o + D_MODEL)).astype(BF16)
        gs_ref[rows, :] = _sigmoid(proj(o + D_MODEL, o + 2 * D_MODEL)).astype(BF16)


def _inproj(x, sc1, sh1, norm_mix, w_in_bf, qg, kg, ones_bd, bt):
    b, n, d = x.shape
    tok = lambda w: pl.BlockSpec((None, bt, w), lambda i, j: (i, j, 0))
    mod = pl.BlockSpec((None, 1, d), lambda i, j: (i, 0, 0))
    full = lambda shape: pl.BlockSpec(shape, lambda i, j: tuple(0 for _ in shape))
    return pl.pallas_call(
        _inproj_kernel,
        grid=(b, n // bt),
        in_specs=[tok(d), mod, mod, full((1, d)), full((d, D_IN)),
                  full((1, D_ATTN)), full((1, D_ATTN)), full((D_ATTN, D_ATTN))],
        out_specs=[tok(D_ATTN), tok(D_ATTN), tok(D_ATTN),
                   pl.BlockSpec((None, N_SG, bt // SSM_L, SSM_L * LANES), lambda i, j: (i, 0, j, 0)),
                   tok(d), tok(d)],
        out_shape=[jax.ShapeDtypeStruct((b, n, D_ATTN), BF16)] * 3
        + [jax.ShapeDtypeStruct((b, N_SG, n // SSM_L, SSM_L * LANES), BF16)]
        + [jax.ShapeDtypeStruct((b, n, d), BF16)] * 2,
        scratch_shapes=[pltpu.VMEM((d // LANES, bt, LANES), F32)],
        compiler_params=_cparams(("parallel", "parallel")),
        name="inproj",
    )(x, sc1, sh1, norm_mix, w_in_bf, qg, kg, ones_bd)


def _attn_kernel(q_ref, kp_ref, kc_ref, kn_ref, vp_ref, vc_ref, vn_ref, bias_ref, o_ref,
                 kwin, vwin, *, rows):
    r0 = pl.program_id(1) * ATTN_ROWS
    rw = ATTN_ROWS * GRID_W
    for t, (kr, vr) in enumerate(((kp_ref, vp_ref), (kc_ref, vc_ref), (kn_ref, vn_ref))):
        kwin[t * rw:(t + 1) * rw, :] = kr[...]
        vwin[t * rw:(t + 1) * rw, :] = vr[...]
    even = lax.broadcasted_iota(I32, (GRID_W, LANES), 1) < HEAD_DIM
    nkeys = WIN_R * GRID_W
    pairs = N_HEADS // 2

    def rows_body(it, carry):
        units = []
        for sub in range(ATTN_UNROLL):
            i = it * ATTN_UNROLL + sub
            r = r0 + i
            rs = jnp.clip(r - WIN_R // 2, 0, rows - WIN_R)
            variant = r - rs
            koff = pl.multiple_of((rs - r0 + ATTN_ROWS) * GRID_W, GRID_W)
            qoff = pl.multiple_of(i * GRID_W, GRID_W)
            for hp in range(pairs):
                ls = slice(hp * LANES, (hp + 1) * LANES)
                qp = q_ref[pl.ds(qoff, GRID_W), ls]
                zero = jnp.zeros_like(qp)
                q2 = jnp.concatenate([jnp.where(even, qp, zero), jnp.where(even, zero, qp)], axis=0)
                kp = kwin[pl.ds(koff, nkeys), ls]
                s = lax.dot_general(q2, kp, (((1,), (1,)), ((), ())), preferred_element_type=F32)
                units.append((qoff, koff, ls, s + bias_ref[variant, hp]))
        probs = []
        for qoff, koff, ls, s in units:
            p = jnp.exp(s - jnp.max(s, axis=-1, keepdims=True))
            probs.append((p.astype(BF16), jnp.sum(p, axis=-1, keepdims=True)))
        for (qoff, koff, ls, _), (p, l) in zip(units, probs):
            o2 = _dot(p, vwin[pl.ds(koff, nkeys), ls]) / l
            o_ref[pl.ds(qoff, GRID_W), ls] = jnp.where(even, o2[:GRID_W], o2[GRID_W:]).astype(BF16)
        return carry

    lax.fori_loop(0, ATTN_ROWS // ATTN_UNROLL, rows_body, 0)


def _attention(q, k, v, bias_tab):
    b, n, _ = q.shape
    rows = n // GRID_W
    assert rows % ATTN_ROWS == 0 and rows >= WIN_R
    nblk = rows // ATTN_ROWS
    rw = ATTN_ROWS * GRID_W
    cur = pl.BlockSpec((None, rw, D_ATTN), lambda i, j: (i, j, 0))
    prv = pl.BlockSpec((None, rw, D_ATTN), lambda i, j: (i, jnp.maximum(j - 1, 0), 0))
    nxt = pl.BlockSpec((None, rw, D_ATTN), lambda i, j: (i, jnp.minimum(j + 1, nblk - 1), 0))
    return pl.pallas_call(
        functools.partial(_attn_kernel, rows=rows),
        grid=(b, nblk),
        in_specs=[cur, prv, cur, nxt, prv, cur, nxt,
                  pl.BlockSpec(bias_tab.shape, lambda i, j: (0, 0, 0, 0))],
        out_specs=cur,
        out_shape=jax.ShapeDtypeStruct((b, n, D_ATTN), BF16),
        scratch_shapes=[pltpu.VMEM((3 * rw, D_ATTN), BF16), pltpu.VMEM((3 * rw, D_ATTN), BF16)],
        compiler_params=_cparams(("parallel", "parallel")),
        name="attn",
    )(q, k, k, k, v, v, v, bias_tab)


def _attn_bias_table(rpb):
    var = jnp.arange(WIN_R)
    a = jnp.arange(WIN_R)
    j = jnp.arange(GRID_W)
    c_start = jnp.clip(j - WIN_C // 2, 0, GRID_W - WIN_C)
    col_ok = (j[None, :] >= c_start[:, None]) & (j[None, :] < c_start[:, None] + WIN_C)
    col_off = jnp.clip(j[None, :] - j[:, None], -(WIN_C - 1), WIN_C - 1) + WIN_C - 1
    row_off = a[None, :] - var[:, None] + WIN_R - 1
    hp = lax.Precision.HIGHEST
    row_sel = (row_off[:, :, None] == jnp.arange(2 * WIN_R - 1)).astype(F32)
    col_sel = (col_off[:, :, None] == jnp.arange(2 * WIN_C - 1)).astype(F32)
    tab = jnp.einsum('hrc,var->hvac', rpb.astype(F32), row_sel, precision=hp)
    tab = jnp.einsum('hvac,jkc->vhjak', tab, col_sel, precision=hp)
    tab = jnp.where(col_ok[None, None, :, None, :], tab, NEG_INF)
    return tab.reshape(WIN_R, N_HEADS // 2, 2 * GRID_W, WIN_R * GRID_W)


def _ssm_tables(a_re, a_im, log_dt, b_re, b_im, c_re, c_im, d_skip):
    L = SSM_L
    hp = lax.Precision.HIGHEST
    lam_re, lam_im = a_re.astype(F32), a_im.astype(F32)
    dt = jnp.exp(log_dt.astype(F32))[..., None]
    ldt_re, ldt_im = lam_re * dt, lam_im * dt

    def apow(kk):
        mag = jnp.exp(ldt_re * kk)
        return mag * jnp.cos(ldt_im * kk), mag * jnp.sin(ldt_im * kk)

    pw_re, pw_im = apow(jnp.arange(L + 1, dtype=F32)[:, None, None, None])
    den = lam_re * lam_re + lam_im * lam_im
    co_re = ((pw_re[1] - 1.0) * lam_re + pw_im[1] * lam_im) / den
    co_im = (pw_im[1] * lam_re - (pw_re[1] - 1.0) * lam_im) / den
    bm_re, bm_im = b_re.astype(F32), b_im.astype(F32)
    bb_re = co_re[..., None] * bm_re - co_im[..., None] * bm_im
    bb_im = co_re[..., None] * bm_im + co_im[..., None] * bm_re
    cm_re, cm_im = c_re.astype(F32), c_im.astype(F32)

    e_re = cm_re[None] * pw_re[:L, :, :, None, :] - cm_im[None] * pw_im[:L, :, :, None, :]
    e_im = cm_re[None] * pw_im[:L, :, :, None, :] + cm_im[None] * pw_re[:L, :, :, None, :]
    kern = (jnp.einsum('ldgxp,dgpc->ldgxc', e_re, bb_re, precision=hp)
            - jnp.einsum('ldgxp,dgpc->ldgxc', e_im, bb_im, precision=hp))
    kern = kern.reshape(L, 2, N_SG, SG_GROUPS, SSM_GROUP, SSM_GROUP)
    s_idx = jnp.arange(L)[:, None]
    t_idx = jnp.arange(L)[None, :]
    lag = jnp.arange(L)
    sel_f = ((t_idx - s_idx)[:, :, None] == lag).astype(F32)
    sel_b = ((s_idx - t_idx)[:, :, None] == lag).astype(F32)
    t_small = (jnp.einsum('stl,lqgxc->qsgctx', sel_f, kern[:, 0], precision=hp)
               + jnp.einsum('stl,lqgxc->qsgctx', sel_b, kern[:, 1], precision=hp))
    t_small = t_small.reshape(N_SG, L * LANES, LANES)

    def cmul(p_re, p_im, w_re, w_im):
        return p_re * w_re - p_im * w_im, p_re * w_im + p_im * w_re

    dn_re, dn_im = apow(L - jnp.arange(L + 1, dtype=F32)[:, None, None, None])

    def powers(lo, d, descending):
        if descending:
            return dn_re[L + 1 - lo - L:L + 1 - lo, d], dn_im[L + 1 - lo - L:L + 1 - lo, d]
        return pw_re[lo:lo + L, d], pw_im[lo:lo + L, d]

    parts = []
    for d, flip in ((0, True), (1, False)):
        p_re, p_im = powers(0, d, flip)
        parts += list(cmul(p_re[:, :, None, :], p_im[:, :, None, :],
                           bb_re[d].transpose(0, 2, 1)[None], bb_im[d].transpose(0, 2, 1)[None]))
    wb_small = jnp.stack(parts, axis=3)
    wb_small = wb_small.reshape(L, N_SG, SG_GROUPS, SSM_GROUP, 4 * STATE_P).transpose(1, 0, 2, 3, 4)
    wb_small = wb_small.reshape(N_SG, L * LANES, 4 * STATE_P)

    parts = []
    for d, flip in ((0, False), (1, True)):
        p_re, p_im = powers(1, d, flip)
        z_re, z_im = cmul(p_re[:, :, None, :], p_im[:, :, None, :], cm_re[d][None], cm_im[d][None])
        parts += [z_re, -z_im]
    wc_small = jnp.stack(parts, axis=0).reshape(4, L, N_SG, SG_GROUPS, SSM_GROUP, STATE_P)
    wc_small = wc_small.transpose(2, 0, 3, 5, 1, 4).reshape(N_SG, 4 * SG_STATE, LANES)

    lane = jnp.arange(LANES)
    col = jnp.arange(L * LANES)
    rep_lane = ((lane[:, None] // SSM_GROUP == col[None, :] // LANES)
                & (lane[:, None] % SSM_GROUP == col[None, :] % SSM_GROUP))
    st = jnp.arange(4 * STATE_P)
    scol = jnp.arange(4 * SG_STATE)
    rep_state = ((st[:, None] // STATE_P == scol[None, :] // SG_STATE)
                 & (st[:, None] % STATE_P == scol[None, :] % STATE_P))
    lane_group = (col % LANES) // SSM_GROUP
    state_group = (scol % SG_STATE) // STATE_P

    def expand(small, rep, row_group, col_group):
        big = jnp.einsum('qrk,kc->qrc', small.astype(BF16), rep.astype(BF16), preferred_element_type=F32)
        return jnp.where(row_group[:, None] == col_group[None, :], big, 0.0).astype(BF16)

    t_mat = expand(t_small, rep_lane, lane_group, lane_group)
    wb_mat = expand(wb_small, rep_state, lane_group, state_group)
    wc_mat = expand(wc_small, rep_lane, state_group, lane_group)

    row = jnp.arange(SCAN_ROWS)
    sh = jnp.array([1, 2, 4])
    ones = jnp.ones((1, SCAN_ROWS), bool)
    keep = jnp.stack([jnp.concatenate([row[None, :] >= sh[:, None], ones]),
                      jnp.concatenate([row[None, :] <= SCAN_ROWS - 1 - sh[:, None], ones])])
    shifts = jnp.broadcast_to(sh[:, None], (3, SCAN_ROWS))
    expo = L * jnp.stack([jnp.concatenate([shifts, row[None, :] + 1]),
                          jnp.concatenate([shifts, SCAN_ROWS - row[None, :]])]).astype(F32)
    e = expo[:, :, :, None, None]
    mag = jnp.exp(ldt_re[:, None, None] * e)
    k5 = keep[:, :, :, None, None]
    cst = jnp.stack([jnp.where(k5, mag * jnp.cos(ldt_im[:, None, None] * e), 0.0),
                     jnp.where(k5, mag * jnp.sin(ldt_im[:, None, None] * e), 0.0)], axis=2)
    cst = cst.reshape(2, 4, 2, SCAN_ROWS, N_SG, SG_STATE).transpose(4, 0, 1, 2, 3, 5)

    dsk = jnp.tile(d_skip.astype(F32).reshape(N_SG, 1, LANES), (1, 1, L))
    return t_mat, wb_mat, wc_mat, cst, dsk


def _ssm_kernel(u_ref, t_ref, wb_ref, wc_ref, cst_ref, d_ref, y_ref, stf_ref, stb_ref, *, n_chunks, mm_rows):
    n_mm = n_chunks // mm_rows
    tiles = mm_rows // SCAN_ROWS
    half = SG_STATE
    row_id = lax.broadcasted_iota(I32, (SCAN_ROWS, half), 0)
    zero = jnp.zeros((1, half), F32)

    def block(c):
        return pl.ds(pl.multiple_of(c * mm_rows, mm_rows), mm_rows)

    def scan_tile(st_ref, d, row0, carry):
        rows = pl.ds(pl.multiple_of(row0, SCAN_ROWS), SCAN_ROWS)
        xr = st_ref[rows, 0:half]
        xi = st_ref[rows, half:2 * half]
        for si, sh in enumerate((1, 2, 4)):
            ar, ai = cst_ref[d, si, 0], cst_ref[d, si, 1]
            shift = sh if d == 0 else SCAN_ROWS - sh
            pr, pi = pltpu.roll(xr, shift, 0), pltpu.roll(xi, shift, 0)
            xr, xi = xr + (ar * pr - ai * pi), xi + (ar * pi + ai * pr)
        cr, ci = carry
        ar, ai = cst_ref[d, 3, 0], cst_ref[d, 3, 1]
        xr, xi = xr + (ar * cr - ai * ci), xi + (ar * ci + ai * cr)
        if d == 0:
            edge, shift, last = 0, 1, SCAN_ROWS - 1
        else:
            edge, shift, last = SCAN_ROWS - 1, SCAN_ROWS - 1, 0
        st_ref[rows, 0:half] = jnp.where(row_id == edge, cr, pltpu.roll(xr, shift, 0))
        st_ref[rows, half:2 * half] = jnp.where(row_id == edge, ci, pltpu.roll(xi, shift, 0))
        return xr[last:last + 1, :], xi[last:last + 1, :]

    def inject_fwd(c, carry):
        stf_ref[block(c), :] = _dot(u_ref[block(c), :], wb_ref[:, 0:2 * half])
        return carry

    lax.fori_loop(0, n_mm, inject_fwd, 0)

    def forward(c, carry):
        u = u_ref[block(c), :]
        y_ref[block(c), :] = _dot(u, t_ref[...]) + u.astype(F32) * d_ref[...]
        stb_ref[block(c), :] = _dot(u, wb_ref[:, 2 * half:4 * half])
        for k in range(tiles):
            carry = scan_tile(stf_ref, 0, c * mm_rows + k * SCAN_ROWS, carry)
        return carry

    lax.fori_loop(0, n_mm, forward, (zero, zero))

    def backward(i, carry):
        c = n_mm - 1 - i
        y_ref[block(c), :] += _dot(stf_ref[block(c), :].astype(BF16), wc_ref[0:2 * half, :])
        for k in reversed(range(tiles)):
            carry = scan_tile(stb_ref, 1, c * mm_rows + k * SCAN_ROWS, carry)
        return carry

    lax.fori_loop(0, n_mm, backward, (zero, zero))

    def eject_bwd(c, carry):
        y_ref[block(c), :] += _dot(stb_ref[block(c), :].astype(BF16), wc_ref[2 * half:4 * half, :])
        return carry

    lax.fori_loop(0, n_mm, eject_bwd, 0)


def _ssm(u, tabs):
    t_mat, wb_mat, wc_mat, cst, dsk = tabs
    b, _, n_chunks, lw = u.shape
    mm_rows = min(256, n_chunks)
    assert n_chunks % mm_rows == 0 and n_chunks % SCAN_ROWS == 0
    one = pl.Buffered(1)
    wspec = lambda shape: pl.BlockSpec((None,) + shape, lambda q, i: (q,) + tuple(0 for _ in shape),
                                       pipeline_mode=one)
    return pl.pallas_call(
        functools.partial(_ssm_kernel, n_chunks=n_chunks, mm_rows=mm_rows),
        grid=(N_SG, b),
        in_specs=[pl.BlockSpec((None, None, n_chunks, lw), lambda q, i: (i, q, 0, 0)),
                  wspec((lw, lw)), wspec((lw, 4 * SG_STATE)), wspec((4 * SG_STATE, lw)),
                  wspec((2, 4, 2, SCAN_ROWS, SG_STATE)), wspec((1, lw))],
        out_specs=pl.BlockSpec((None, None, n_chunks, lw), lambda q, i: (i, q, 0, 0)),
        out_shape=jax.ShapeDtypeStruct((b, N_SG, n_chunks, lw), F32),
        scratch_shapes=[pltpu.VMEM((n_chunks, 2 * SG_STATE), F32)] * 2,
        compiler_params=_cparams(("arbitrary", "arbitrary")),
        name="ssm",
    )(u, t_mat, wb_mat, wc_mat, cst, dsk)


def _merge_kernel(x_ref, attn_ref, y_ref, ga_ref, gs_ref, g1_ref, sc_ref, sh_ref, nf_ref,
                  wab_ref, wglu_ref, wout_ref, wr_ref, x1_ref, h2_ref, lg_ref, gel_scr):
    sub = x_ref.shape[0] // MERGE_SPLIT
    crows = sub // SSM_L
    blocks = [(k, slice(k * sub, (k + 1) * sub)) for k in range(MERGE_SPLIT)]

    ab = [_dot(attn_ref[rows, :], wab_ref[...]) for _, rows in blocks]

    gel = []
    for k, rows in blocks:
        for sg in range(N_SG):
            for s in range(SSM_L):
                y = y_ref[sg, k * crows:(k + 1) * crows, s * LANES:(s + 1) * LANES]
                gel_scr[sg, k * sub + s * crows:k * sub + (s + 1) * crows, :] = (
                    0.5 * y * (1.0 + jnp.tanh(math.sqrt(2.0 / math.pi) * (y + 0.044715 * (y * y * y)))))
        gel.append(jnp.concatenate(
            [jnp.concatenate([gel_scr[sg, pl.ds(k * sub + j, SSM_L, stride=crows), :] for j in range(crows)], axis=0)
             for sg in range(N_SG)], axis=1).astype(BF16))
    glu = [_dot(g, wglu_ref[...]) for g in gel]
    merged = []
    for (k, rows), ab_k, glu_k in zip(blocks, ab, glu):
        sb = glu_k[:, :D_MODEL] * _sigmoid(glu_k[:, D_MODEL:])
        merged.append((ga_ref[rows, :].astype(F32) * ab_k + gs_ref[rows, :].astype(F32) * sb).astype(BF16))
    mixed = [_dot(m, wout_ref[...]) for m in merged]
    h2s = []
    for (k, rows), mix in zip(blocks, mixed):
        x1 = x_ref[rows, :] + g1_ref[...] * mix
        x1_ref[rows, :] = x1
        ms = jnp.mean(x1 * x1, axis=-1, keepdims=True)
        h2 = x1 * lax.rsqrt(ms + EPS) * nf_ref[...] * (1.0 + sc_ref[...]) + sh_ref[...]
        h2_ref[rows, :D_MODEL] = h2.astype(BF16)
        h2s.append(h2)
    for (k, rows), h2 in zip(blocks, h2s):
        lg = _dot3(h2, wr_ref[...])
        lg_ref[rows, :] = lg
        valid = lax.broadcasted_iota(I32, lg.shape, 1) < N_EXPERTS
        m = jnp.max(jnp.where(valid, lg, -jnp.inf), axis=-1, keepdims=True)
        ex = jnp.where(valid, jnp.exp(lg - m), 0.0)
        aff = ex / jnp.sum(ex, axis=-1, keepdims=True)
        hi = aff.astype(BF16).astype(F32)
        mid = (aff - hi).astype(BF16).astype(F32)
        lo = (aff - hi - mid).astype(BF16).astype(F32)
        parts = hi + pltpu.roll(mid, N_EXPERTS, 1) + pltpu.roll(lo, 2 * N_EXPERTS, 1)
        h2_ref[rows, D_MODEL:] = parts.astype(BF16)


def _merge(x, attn, y4, ga, gs, g1, sc2, sh2, norm_ffn, wab, wglu, wout, wr_pad, bt):
    b, n, d = x.shape
    tok = lambda w: pl.BlockSpec((None, bt, w), lambda i, j: (i, j, 0))
    mod = pl.BlockSpec((None, 1, d), lambda i, j: (i, 0, 0))
    full = lambda shape: pl.BlockSpec(shape, lambda i, j: tuple(0 for _ in shape))
    return pl.pallas_call(
        _merge_kernel,
        grid=(b, n // bt),
        in_specs=[tok(d), tok(D_ATTN),
                  pl.BlockSpec((None, N_SG, bt // SSM_L, SSM_L * LANES), lambda i, j: (i, 0, j, 0)),
                  tok(d), tok(d), mod, mod, mod, full((1, d)),
                  full((D_ATTN, d)), full((D_SSM, 2 * d)), full((d, d)), full((d, LANES))],
        out_specs=[tok(d), tok(d + LANES), tok(LANES)],
        out_shape=[jax.ShapeDtypeStruct((b, n, d), F32), jax.ShapeDtypeStruct((b, n, d + LANES), BF16),
                   jax.ShapeDtypeStruct((b, n, LANES), F32)],
        scratch_shapes=[pltpu.VMEM((N_SG, bt, LANES), F32)],
        compiler_params=_cparams(("parallel", "parallel")),
        name="merge",
    )(x, attn, y4, ga, gs, g1, sc2, sh2, norm_ffn, wab, wglu, wout, wr_pad)


def _route_kernel(lg_ref, tri_ref, slot_ref, cnt_ref, aff_ref, *, cap):
    lg = lg_ref[...]
    m = jnp.max(lg, axis=1, keepdims=True)
    e = jnp.exp(lg - m)
    aff = e / jnp.sum(e, axis=1, keepdims=True)
    aff_ref[...] = aff

    def count(mask):
        c = jnp.sum(jnp.where(mask, 1.0, 0.0), axis=0, keepdims=True)
        return jnp.sum(c, axis=2, keepdims=True)

    def as_float(bits):
        return lax.bitcast_convert_type(bits, F32)

    def bit_step(i, thr):
        cand = thr | (jnp.int32(1) << (30 - i))
        return jnp.where(count(aff >= as_float(cand)) >= cap, cand, thr)

    thr3 = as_float(lax.fori_loop(0, 31, bit_step, jnp.zeros((1, N_EXPERTS, 1), I32)))
    need = (cap - count(aff > thr3))[0]
    thr = thr3[0]
    n_chunks = lg.shape[0]
    carry0 = jnp.zeros((N_EXPERTS, 1), F32)

    def prefix(flag, carry):
        f = jnp.where(flag, 1.0, 0.0)
        inc = _dot(f.astype(BF16), tri_ref[...]) + carry
        return inc, inc - f

    def tie_body(c, carry):
        a = aff_ref[c]
        tie = a == thr
        inc, rank = prefix(tie, carry)
        sel = (a > thr) | (tie & (rank < need))
        slot_ref[c] = jnp.where(sel, 1, 0).astype(I32)
        return inc[:, CUM_W - 1:CUM_W]

    lax.fori_loop(0, n_chunks, tie_body, carry0)

    def slot_body(c, carry):
        sel = slot_ref[c] > 0
        inc, excl = prefix(sel, carry)
        slot_ref[c] = jnp.where(sel, excl, -1.0).astype(I32)
        cnt_ref[c] = excl.astype(I32)
        return inc[:, CUM_W - 1:CUM_W]

    lax.fori_loop(0, n_chunks, slot_body, carry0)


def _route(logits_c, tri, cap):
    nc = logits_c.shape[0]
    shp = (nc, N_EXPERTS, CUM_W)
    return pl.pallas_call(
        functools.partial(_route_kernel, cap=cap),
        grid=(1,),
        in_specs=[pl.BlockSpec(shp, lambda i: (0, 0, 0)), pl.BlockSpec((CUM_W, CUM_W), lambda i: (0, 0))],
        out_specs=[pl.BlockSpec(shp, lambda i: (0, 0, 0))] * 2,
        out_shape=[jax.ShapeDtypeStruct(shp, I32)] * 2,
        scratch_shapes=[pltpu.VMEM(shp, F32)],
        compiler_params=_cparams(("arbitrary",)),
        name="route",
    )(logits_c, tri)


def _gather_kernel(offs_ref, h_ref, slot_ref, xe_ref, *, nb, cap, win):
    eg = pl.program_id(0)
    blk = pl.program_id(1)

    @pl.when(blk == 0)
    def _():
        xe_ref[...] = jnp.zeros_like(xe_ref)

    riota = lax.broadcasted_iota(I32, (win, 1), 0)

    def window(ee, w):
        off = offs_ref[(eg * GATHER_EXPERTS + ee) * (nb + 1) + blk]
        nominal = (off // BF16_ROWS) * BF16_ROWS + w * win
        start = pl.multiple_of(jnp.minimum(nominal, cap - win), BF16_ROWS)
        slot = slot_ref[ee]
        hit = (slot == riota + start) & (slot >= nominal)
        return jnp.where(hit, 1.0, 0.0).astype(BF16), start

    firsts = [window(ee, 0) for ee in range(GATHER_EXPERTS)]
    picked = _dot(jnp.concatenate([oh for oh, _ in firsts], axis=0), h_ref[...]).astype(BF16)
    for ee, (_, start) in enumerate(firsts):
        xe_ref[ee, pl.ds(start, win), :] += picked[ee * win:(ee + 1) * win]

    for ee in range(GATHER_EXPERTS):
        off = offs_ref[(eg * GATHER_EXPERTS + ee) * (nb + 1) + blk]
        end = offs_ref[(eg * GATHER_EXPERTS + ee) * (nb + 1) + blk + 1]
        n_win = (end - (off // BF16_ROWS) * BF16_ROWS + win - 1) // win

        def extra(w, carry, ee=ee):
            onehot, start = window(ee, w)
            xe_ref[ee, pl.ds(start, win), :] += _dot(onehot, h_ref[...]).astype(BF16)
            return carry

        lax.fori_loop(1, n_win, extra, 0)


def _gather(offs_flat, h2, slot3, cap, bt):
    t, d = h2.shape
    nb = t // bt
    win = min(GATHER_WIN, cap)
    ge = GATHER_EXPERTS
    gs = pltpu.PrefetchScalarGridSpec(
        num_scalar_prefetch=1,
        grid=(N_EXPERTS // ge, nb),
        in_specs=[pl.BlockSpec((bt, d), lambda e, j, offs: (j, 0)),
                  pl.BlockSpec((ge, 1, bt), lambda e, j, offs: (e, 0, j))],
        out_specs=pl.BlockSpec((ge, cap, d), lambda e, j, offs: (e, 0, 0), pipeline_mode=pl.Buffered(1)),
    )
    return pl.pallas_call(
        functools.partial(_gather_kernel, nb=nb, cap=cap, win=win),
        grid_spec=gs,
        out_shape=jax.ShapeDtypeStruct((N_EXPERTS, cap, d), BF16),
        compiler_params=_cparams(("arbitrary", "arbitrary")),
        name="gather",
    )(offs_flat, h2, slot3)


def _ffn_kernel(x_ref, wg_ref, wu_ref, wd_ref, y_ref, *, fchunk):
    x = x_ref[:, :D_MODEL]
    acc = jnp.zeros((x.shape[0], D_MODEL), F32)
    for f in range(D_EXPERT // fchunk):
        fs = slice(f * fchunk, (f + 1) * fchunk)
        a = _dot(x, wg_ref[:, fs])
        u = _dot(x, wu_ref[:, fs])
        hmid = (a * _sigmoid(a) * u).astype(BF16)
        acc = acc + _dot(hmid, wd_ref[fs, :])
    parts = x_ref[:, D_MODEL:].astype(F32)
    lane = lax.broadcasted_iota(I32, parts.shape, 1)
    mine = (lane % N_EXPERTS == pl.program_id(0)) & (lane < 3 * N_EXPERTS)
    gate = jnp.sum(jnp.where(mine, parts, 0.0), axis=-1, keepdims=True)
    y_ref[...] = (gate * acc).astype(BF16)


def _ffn(xe, wg, wu, wd, tm):
    e, cap, dx = xe.shape
    d = D_MODEL
    return pl.pallas_call(
        functools.partial(_ffn_kernel, fchunk=512),
        grid=(e, cap // tm),
        in_specs=[pl.BlockSpec((None, tm, dx), lambda i, j: (i, j, 0)),
                  pl.BlockSpec((None, d, D_EXPERT), lambda i, j: (i, 0, 0)),
                  pl.BlockSpec((None, d, D_EXPERT), lambda i, j: (i, 0, 0)),
                  pl.BlockSpec((None, D_EXPERT, d), lambda i, j: (i, 0, 0))],
        out_specs=pl.BlockSpec((None, tm, d), lambda i, j: (i, j, 0)),
        out_shape=jax.ShapeDtypeStruct((e, cap, d), BF16),
        compiler_params=_cparams(("parallel", "parallel")),
        name="ffn",
    )(xe, wg, wu, wd)


def _window_copy(ye_hbm, buf, sem, e, start, win):
    return pltpu.make_async_copy(ye_hbm.at[e, pl.ds(start, win), :], buf, sem)


def _combine_kernel(offs_ref, x1_ref, slot_ref, g2_ref, ye_hbm, o_ref, ybuf, xbuf, lhs, sems, xsem,
                    *, nb, cap, win):
    blk = pl.program_id(0)

    def first_window(b_, e):
        off = offs_ref[e * (nb + 1) + b_]
        a0 = (off // BF16_ROWS) * BF16_ROWS
        return a0, pl.multiple_of(jnp.minimum(a0, cap - win), BF16_ROWS)

    def window_copies(b_, half):
        return [_window_copy(ye_hbm, ybuf.at[half, pl.ds(e * win, win)], sems.at[half, e], e,
                             first_window(b_, e)[1], win) for e in range(N_EXPERTS)]

    @pl.when(blk == 0)
    def _():
        for cp in window_copies(0, 0):
            cp.start()

    @pl.when(blk + 1 < nb)
    def _():
        for cp in window_copies(blk + 1, (blk + 1) % 2):
            cp.start()

    starts = [first_window(blk, e) for e in range(N_EXPERTS)]
    liota = lax.broadcasted_iota(I32, (1, win), 1)
    slots = slot_ref[...]
    for e in range(N_EXPERTS):
        hit = slots[:, e:e + 1] == liota + starts[e][1]
        lhs[:, e * win:(e + 1) * win] = jnp.where(hit, 1.0, 0.0).astype(BF16)
    for cp in window_copies(blk, blk % 2):
        cp.wait()
    o_ref[...] = x1_ref[...] + g2_ref[...] * _dot(lhs[...], ybuf[blk % 2])

    for e in range(N_EXPERTS):
        a0 = starts[e][0]
        end = offs_ref[e * (nb + 1) + blk + 1]
        n_win = (end - a0 + win - 1) // win

        def extra(w, carry, e=e, a0=a0):
            nominal = a0 + w * win
            st = pl.multiple_of(jnp.minimum(nominal, cap - win), BF16_ROWS)
            cp = _window_copy(ye_hbm, xbuf, xsem, e, st, win)
            cp.start()
            cp.wait()
            scol = slot_ref[:, e:e + 1]
            hit = (scol == liota + st) & (scol >= nominal)
            o_ref[...] += g2_ref[...] * _dot(jnp.where(hit, 1.0, 0.0).astype(BF16), xbuf[...])
            return carry

        lax.fori_loop(1, n_win, extra, 0)


def _combine(offs_flat, x1, slot_t, g2, ye, n_per_batch, bt):
    t, d = x1.shape
    nb = t // bt
    cap = ye.shape[1]
    win = min(SLOT_WIN, cap)
    per = n_per_batch // bt
    gs = pltpu.PrefetchScalarGridSpec(
        num_scalar_prefetch=1,
        grid=(nb,),
        in_specs=[pl.BlockSpec((bt, d), lambda j, offs: (j, 0)),
                  pl.BlockSpec((bt, N_EXPERTS), lambda j, offs: (j, 0)),
                  pl.BlockSpec((None, 1, d), lambda j, offs: (j // per, 0, 0)),
                  pl.BlockSpec(memory_space=pl.ANY)],
        out_specs=pl.BlockSpec((bt, d), lambda j, offs: (j, 0)),
        scratch_shapes=[pltpu.VMEM((2, N_EXPERTS * win, d), BF16), pltpu.VMEM((win, d), BF16),
                        pltpu.VMEM((bt, N_EXPERTS * win), BF16),
                        pltpu.SemaphoreType.DMA((2, N_EXPERTS)), pltpu.SemaphoreType.DMA(())],
    )
    return pl.pallas_call(
        functools.partial(_combine_kernel, nb=nb, cap=cap, win=win),
        grid_spec=gs,
        out_shape=jax.ShapeDtypeStruct((t, d), F32),
        compiler_params=_cparams(("arbitrary",)),
        name="combine",
    )(offs_flat, x1, slot_t, g2, ye)


def _prep_weights(w_in, q_norm, k_norm, rpb, ssm_params, w_glu, w_attn_br, w_out, w_router,
                  w_exp_gate, w_exp_up, w_exp_down):
    head = jnp.arange(D_ATTN) // HEAD_DIM
    return dict(
        w_in=w_in.astype(BF16),
        qg=jnp.tile(q_norm.astype(F32), N_HEADS).reshape(1, D_ATTN),
        kg=jnp.tile(k_norm.astype(F32), N_HEADS).reshape(1, D_ATTN),
        ones_bd=(head[:, None] == head[None, :]).astype(BF16),
        bias_tab=_attn_bias_table(rpb),
        ssm=_ssm_tables(*ssm_params),
        wglu=w_glu.astype(BF16), wab=w_attn_br.astype(BF16), wout=w_out.astype(BF16),
        wr=jnp.pad(w_router.astype(F32), ((0, 0), (0, LANES - N_EXPERTS))),
        wg=w_exp_gate.astype(BF16), wu=w_exp_up.astype(BF16), wd=w_exp_down.astype(BF16),
        tri=(jnp.arange(CUM_W)[:, None] <= jnp.arange(CUM_W)[None, :]).astype(BF16),
    )


def _token_block(n, want):
    bt = min(want, n)
    assert n % bt == 0
    return bt


def _encoder_layer(x, c, w_ada, b_ada, norm_mix, norm_ffn, wts):
    b, n, d = x.shape
    t = b * n
    cap = EC_CAPACITY * t // N_EXPERTS

    c_pad = jnp.pad(c.astype(F32), ((0, (-b) % SUBLANES), (0, 0)))
    mod = _ada(c_pad, w_ada, b_ada)[:b]
    sh1, sc1, g1, sh2, sc2, g2 = [m.reshape(b, 1, d) for m in jnp.split(mod, 6, axis=-1)]

    bt = _token_block(n, 512)
    q, k, v, u4, ga, gs = _inproj(x, sc1, sh1, norm_mix.reshape(1, d), wts["w_in"], wts["qg"], wts["kg"],
                                  wts["ones_bd"], bt)
    attn = _attention(q, k, v, wts["bias_tab"])
    y4 = _ssm(u4, wts["ssm"])
    x1, h2, logits = _merge(x, attn, y4, ga, gs, g1, sc2, sh2, norm_ffn.reshape(1, d),
                            wts["wab"], wts["wglu"], wts["wout"], wts["wr"], bt)

    lg = logits.reshape(t, LANES)[:, :N_EXPERTS]
    lg_c = lg.reshape(t // CUM_W, CUM_W, N_EXPERTS).transpose(0, 2, 1)
    slot_c, cnt_c = _route(lg_c, wts["tri"], cap)
    slot_et = slot_c.transpose(1, 0, 2).reshape(N_EXPERTS, t)
    slot_te = slot_et.T

    bt2 = _token_block(t, 512)
    cnt_at_block = cnt_c.transpose(1, 0, 2).reshape(N_EXPERTS, t)[:, ::bt2]
    offs = jnp.concatenate([cnt_at_block, jnp.full((N_EXPERTS, 1), cap, I32)], axis=-1)
    offs_flat = offs.reshape(-1).astype(I32)

    xe = _gather(offs_flat, h2.reshape(t, d + LANES), slot_et.reshape(N_EXPERTS, 1, t), cap, bt2)
    ye = _ffn(xe, wts["wg"], wts["wu"], wts["wd"], _token_block(cap, 1024))
    out = _combine(offs_flat, x1.reshape(t, d), slot_te, g2, ye, n, bt2)
    return out.reshape(b, n, d)


def kernel(x_prompt, x_sample, c_prompt, c_sample, w_ada, b_ada, norm_mix, norm_ffn, w_in, q_norm, k_norm, rpb,
           ssm_a_re, ssm_a_im, ssm_log_dt, ssm_b_re, ssm_b_im, ssm_c_re, ssm_c_im, ssm_d, w_glu, w_attn_br,
           w_out, w_router, w_exp_gate, w_exp_up, w_exp_down):
    y_prompt, y_sample = x_prompt, x_sample
    for layer in range(w_ada.shape[0]):
        ssm_params = tuple(p[layer] for p in (ssm_a_re, ssm_a_im, ssm_log_dt, ssm_b_re, ssm_b_im,
                                              ssm_c_re, ssm_c_im, ssm_d))
        wts = _prep_weights(w_in[layer], q_norm[layer], k_norm[layer], rpb[layer], ssm_params, w_glu[layer],
                            w_attn_br[layer], w_out[layer], w_router[layer], w_exp_gate[layer],
                            w_exp_up[layer], w_exp_down[layer])
        y_prompt = _encoder_layer(y_prompt, c_prompt, w_ada[layer], b_ada[layer], norm_mix[layer],
                                  norm_ffn[layer], wts)
        y_sample = _encoder_layer(y_sample, c_sample, w_ada[layer], b_ada[layer], norm_mix[layer],
                                  norm_ffn[layer], wts)
    return (y_prompt, y_sample)
```

```python
import functools
import math

import jax
import jax.numpy as jnp
from jax import lax
from jax.experimental import pallas as pl
from jax.experimental.pallas import tpu as pltpu

F32 = jnp.float32
BF16 = jnp.bfloat16
I32 = jnp.int32

D_MODEL = 1024
GRID_W = 64
N_HEADS = 8
HEAD_DIM = 64
D_ATTN = N_HEADS * HEAD_DIM
WIN_R = 8
WIN_C = 16
SSM_GROUP = 16
D_SSM = 512
N_GROUPS = D_SSM // SSM_GROUP
STATE_P = 64
D_IN = 3 * D_ATTN + D_SSM + 2 * D_MODEL
N_EXPERTS = 16
EC_CAPACITY = 2
D_EXPERT = 2048
EPS = 1e-6
NEG_INF = -1e9

LANES = 128
SUBLANES = 8
BF16_ROWS = 16
VMEM_LIMIT = 56 * 1024 * 1024

SSM_L = 8
SG_GROUPS = LANES // SSM_GROUP
N_SG = N_GROUPS // SG_GROUPS
SG_STATE = SG_GROUPS * STATE_P
HALF_GROUPS = SG_GROUPS // 2
HALF_LANES = LANES // 2
HALF_W = SSM_L * HALF_LANES
HALF_STATE = HALF_GROUPS * STATE_P
SCAN_ROWS = SUBLANES

ATTN_ROWS = 8
ATTN_UNROLL = 4
MERGE_SPLIT = 4
INPROJ_SPLIT = 1
SLOT_WIN = 128
GATHER_WIN = 96
GATHER_EXPERTS = 4
CUM_W = 256


def _cparams(sem):
    return pltpu.CompilerParams(dimension_semantics=sem, vmem_limit_bytes=VMEM_LIMIT)


def _split_bf16(a):
    hi = a.astype(BF16)
    lo = (a - hi.astype(F32)).astype(BF16)
    return hi, lo


def _dot(a, b):
    return jnp.dot(a, b, preferred_element_type=F32)


def _dot3(a, b):
    ah, al = _split_bf16(a)
    bh, bl = _split_bf16(b)
    return _dot(ah, bh) + (_dot(ah, bl) + _dot(al, bh))


def _sigmoid(z):
    return 1.0 / (1.0 + jnp.exp(-z))


def _ada_kernel(c_ref, w_ref, b_ref, o_ref):
    c = c_ref[...]
    s = c * _sigmoid(c)
    o_ref[...] = _dot3(s, w_ref[...]) + b_ref[...]


def _ada(c_pad, w_ada, b_ada):
    rows = c_pad.shape[0]
    n_out = w_ada.shape[1]
    return pl.pallas_call(
        _ada_kernel,
        grid=(n_out // D_MODEL,),
        in_specs=[pl.BlockSpec((rows, D_MODEL), lambda j: (0, 0)),
                  pl.BlockSpec((D_MODEL, D_MODEL), lambda j: (0, j)),
                  pl.BlockSpec((1, D_MODEL), lambda j: (0, j))],
        out_specs=pl.BlockSpec((rows, D_MODEL), lambda j: (0, j)),
        out_shape=jax.ShapeDtypeStruct((rows, n_out), F32),
        compiler_params=_cparams(("arbitrary",)),
        name="ada",
    )(c_pad, w_ada, b_ada.reshape(1, n_out))


def _inproj_kernel(x_ref, sc_ref, sh_ref, nm_ref, w_ref, qg_ref, kg_ref, ones_ref,
                   q_ref, k_ref, v_ref, u_ref, ga_ref, gs_ref, h_scr):
    def head_norm(z, gain):
        ssum = _dot((z * z).astype(BF16), ones_ref[...])
        return z * lax.rsqrt(ssum * (1.0 / HEAD_DIM) + EPS) * gain

    sub = x_ref.shape[0] // INPROJ_SPLIT
    crows = sub // SSM_L
    for kb in range(INPROJ_SPLIT):
        rows = slice(kb * sub, (kb + 1) * sub)
        x = x_ref[rows, :]
        ms = jnp.mean(x * x, axis=-1, keepdims=True)
        xn = x * lax.rsqrt(ms + EPS) * nm_ref[...]
        hf = xn * (1.0 + sc_ref[...]) + sh_ref[...]
        for c in range(D_MODEL // LANES):
            h_scr[c, rows, :] = hf[:, c * LANES:(c + 1) * LANES]
        h = hf.astype(BF16)

        def proj(lo, hi, h=h):
            return _dot(h, w_ref[:, lo:hi])

        q = head_norm(proj(0, D_ATTN), qg_ref[...]) * (HEAD_DIM ** -0.5)
        q_ref[rows, :] = q.astype(BF16)
        k = head_norm(proj(D_ATTN, 2 * D_ATTN), kg_ref[...])
        k_ref[rows, :] = k.astype(BF16)
        v_ref[rows, :] = proj(2 * D_ATTN, 3 * D_ATTN).astype(BF16)
        hp = jnp.concatenate(
            [jnp.concatenate([h_scr[c, pl.ds(kb * sub + s, crows, stride=SSM_L), :]
                              for c in range(D_MODEL // LANES)], axis=1)
             for s in range(SSM_L)], axis=0)
        u = _dot(hp.astype(BF16), w_ref[:, 3 * D_ATTN:3 * D_ATTN + D_SSM])
        low = lax.broadcasted_iota(I32, (crows, LANES), 1) < HALF_LANES
        out_rows = slice(kb * crows, (kb + 1) * crows)
        for sg in range(N_SG):
            for sp in range(SSM_L // 2):
                a = u[(2 * sp) * crows:(2 * sp + 1) * crows, sg * LANES:(sg + 1) * LANES]
                b = u[(2 * sp + 1) * crows:(2 * sp + 2) * crows, sg * LANES:(sg + 1) * LANES]
                u_ref[sg, out_rows, sp * LANES:(sp + 1) * LANES] = (
                    jnp.where(low, a, pltpu.roll(b, HALF_LANES, 1)).astype(BF16))
                u_ref[sg, out_rows, HALF_W + sp * LANES:HALF_W + (sp + 1) * LANES] = (
                    jnp.where(low, pltpu.roll(a, HALF_LANES, 1), b).astype(BF16))
        o = 3 * D_ATTN + D_SSM
        ga_ref[rows, :] = _sigmoid(proj(o, o + D_MODEL)).astype(BF16)
        gs_ref[rows, :] = _sigmoid(proj(o + D_MODEL, o + 2 * D_MODEL)).astype(BF16)


def _inproj(x, sc1, sh1, norm_mix, w_in_bf, qg, kg, ones_bd, bt):
    b, n, d = x.shape
    tok = lambda w: pl.BlockSpec((None, bt, w), lambda i, j: (i, j, 0))
    mod = pl.BlockSpec((None, 1, d), lambda i, j: (i, 0, 0))
    full = lambda shape: pl.BlockSpec(shape, lambda i, j: tuple(0 for _ in shape))
    return pl.pallas_call(
        _inproj_kernel,
        grid=(b, n // bt),
        in_specs=[tok(d), mod, mod, full((1, d)), full((d, D_IN)),
                  full((1, D_ATTN)), full((1, D_ATTN)), full((D_ATTN, D_ATTN))],
        out_specs=[tok(D_ATTN), tok(D_ATTN), tok(D_ATTN),
                   pl.BlockSpec((None, N_SG, bt // SSM_L, SSM_L * LANES), lambda i, j: (i, 0, j, 0)),
                   tok(d), tok(d)],
        out_shape=[jax.ShapeDtypeStruct((b, n, D_ATTN), BF16)] * 3
        + [jax.ShapeDtypeStruct((b, N_SG, n // SSM_L, SSM_L * LANES), BF16)]
        + [jax.ShapeDtypeStruct((b, n, d), BF16)] * 2,
        scratch_shapes=[pltpu.VMEM((d // LANES, bt, LANES), F32)],
        compiler_params=_cparams(("parallel", "parallel")),
        name="inproj",
    )(x, sc1, sh1, norm_mix, w_in_bf, qg, kg, ones_bd)


def _attn_kernel(q_ref, kp_ref, kc_ref, kn_ref, vp_ref, vc_ref, vn_ref, bias_ref, o_ref,
                 kwin, vwin, *, rows):
    r0 = pl.program_id(1) * ATTN_ROWS
    rw = ATTN_ROWS * GRID_W
    for t, (kr, vr) in enumerate(((kp_ref, vp_ref), (kc_ref, vc_ref), (kn_ref, vn_ref))):
        kwin[t * rw:(t + 1) * rw, :] = kr[...]
        vwin[t * rw:(t + 1) * rw, :] = vr[...]
    even = lax.broadcasted_iota(I32, (GRID_W, LANES), 1) < HEAD_DIM
    nkeys = WIN_R * GRID_W
    pairs = N_HEADS // 2

    def rows_body(it, carry):
        units = []
        for sub in range(ATTN_UNROLL):
            i = it * ATTN_UNROLL + sub
            r = r0 + i
            rs = jnp.clip(r - WIN_R // 2, 0, rows - WIN_R)
            variant = r - rs
            koff = pl.multiple_of((rs - r0 + ATTN_ROWS) * GRID_W, GRID_W)
            qoff = pl.multiple_of(i * GRID_W, GRID_W)
            for hp in range(pairs):
                ls = slice(hp * LANES, (hp + 1) * LANES)
                qp = q_ref[pl.ds(qoff, GRID_W), ls]
                zero = jnp.zeros_like(qp)
                q2 = jnp.concatenate([jnp.where(even, qp, zero), jnp.where(even, zero, qp)], axis=0)
                kp = kwin[pl.ds(koff, nkeys), ls]
                s = lax.dot_general(q2, kp, (((1,), (1,)), ((), ())), preferred_element_type=F32)
                units.append((qoff, koff, ls, s + bias_ref[variant, hp]))
        probs = []
        for qoff, koff, ls, s in units:
            p = jnp.exp(s - jnp.max(s, axis=-1, keepdims=True))
            probs.append((p.astype(BF16), jnp.sum(p, axis=-1, keepdims=True)))
        for (qoff, koff, ls, _), (p, l) in zip(units, probs):
            o2 = _dot(p, vwin[pl.ds(koff, nkeys), ls]) / l
            o_ref[pl.ds(qoff, GRID_W), ls] = jnp.where(even, o2[:GRID_W], o2[GRID_W:]).astype(BF16)
        return carry

    lax.fori_loop(0, ATTN_ROWS // ATTN_UNROLL, rows_body, 0)


def _attention(q, k, v, bias_tab):
    b, n, _ = q.shape
    rows = n // GRID_W
    assert rows % ATTN_ROWS == 0 and rows >= WIN_R
    nblk = rows // ATTN_ROWS
    rw = ATTN_ROWS * GRID_W
    cur = pl.BlockSpec((None, rw, D_ATTN), lambda i, j: (i, j, 0))
    prv = pl.BlockSpec((None, rw, D_ATTN), lambda i, j: (i, jnp.maximum(j - 1, 0), 0))
    nxt = pl.BlockSpec((None, rw, D_ATTN), lambda i, j: (i, jnp.minimum(j + 1, nblk - 1), 0))
    return pl.pallas_call(
        functools.partial(_attn_kernel, rows=rows),
        grid=(b, nblk),
        in_specs=[cur, prv, cur, nxt, prv, cur, nxt,
                  pl.BlockSpec(bias_tab.shape, lambda i, j: (0, 0, 0, 0))],
        out_specs=cur,
        out_shape=jax.ShapeDtypeStruct((b, n, D_ATTN), BF16),
        scratch_shapes=[pltpu.VMEM((3 * rw, D_ATTN), BF16), pltpu.VMEM((3 * rw, D_ATTN), BF16)],
        compiler_params=_cparams(("parallel", "parallel")),
        name="attn",
    )(q, k, k, k, v, v, v, bias_tab)


def _attn_bias_table(rpb):
    var = jnp.arange(WIN_R)
    a = jnp.arange(WIN_R)
    j = jnp.arange(GRID_W)
    c_start = jnp.clip(j - WIN_C // 2, 0, GRID_W - WIN_C)
    col_ok = (j[None, :] >= c_start[:, None]) & (j[None, :] < c_start[:, None] + WIN_C)
    col_off = jnp.clip(j[None, :] - j[:, None], -(WIN_C - 1), WIN_C - 1) + WIN_C - 1
    row_off = a[None, :] - var[:, None] + WIN_R - 1
    hp = lax.Precision.HIGHEST
    row_sel = (row_off[:, :, None] == jnp.arange(2 * WIN_R - 1)).astype(F32)
    col_sel = (col_off[:, :, None] == jnp.arange(2 * WIN_C - 1)).astype(F32)
    tab = jnp.einsum('hrc,var->hvac', rpb.astype(F32), row_sel, precision=hp)
    tab = jnp.einsum('hvac,jkc->vhjak', tab, col_sel, precision=hp)
    tab = jnp.where(col_ok[None, None, :, None, :], tab, NEG_INF)
    return tab.reshape(WIN_R, N_HEADS // 2, 2 * GRID_W, WIN_R * GRID_W)


def _ssm_tables(a_re, a_im, log_dt, b_re, b_im, c_re, c_im, d_skip):
    L = SSM_L
    hp = lax.Precision.HIGHEST
    lam_re, lam_im = a_re.astype(F32), a_im.astype(F32)
    dt = jnp.exp(log_dt.astype(F32))[..., None]
    ldt_re, ldt_im = lam_re * dt, lam_im * dt

    def apow(kk):
        mag = jnp.exp(ldt_re * kk)
        return mag * jnp.cos(ldt_im * kk), mag * jnp.sin(ldt_im * kk)

    pw_re, pw_im = apow(jnp.arange(L + 1, dtype=F32)[:, None, None, None])
    den = lam_re * lam_re + lam_im * lam_im
    co_re = ((pw_re[1] - 1.0) * lam_re + pw_im[1] * lam_im) / den
    co_im = (pw_im[1] * lam_re - (pw_re[1] - 1.0) * lam_im) / den
    bm_re, bm_im = b_re.astype(F32), b_im.astype(F32)
    bb_re = co_re[..., None] * bm_re - co_im[..., None] * bm_im
    bb_im = co_re[..., None] * bm_im + co_im[..., None] * bm_re
    cm_re, cm_im = c_re.astype(F32), c_im.astype(F32)

    e_re = cm_re[None] * pw_re[:L, :, :, None, :] - cm_im[None] * pw_im[:L, :, :, None, :]
    e_im = cm_re[None] * pw_im[:L, :, :, None, :] + cm_im[None] * pw_re[:L, :, :, None, :]
    kern = (jnp.einsum('ldgxp,dgpc->ldgxc', e_re, bb_re, precision=hp)
            - jnp.einsum('ldgxp,dgpc->ldgxc', e_im, bb_im, precision=hp))
    kern = kern.reshape(L, 2, N_SG, 2, HALF_GROUPS, SSM_GROUP, SSM_GROUP)
    s_idx = jnp.arange(L)[:, None]
    t_idx = jnp.arange(L)[None, :]
    lag = jnp.arange(L)
    sel_f = ((t_idx - s_idx)[:, :, None] == lag).astype(F32)
    sel_b = ((s_idx - t_idx)[:, :, None] == lag).astype(F32)
    t_small = (jnp.einsum('stl,lqhgxc->qhsgctx', sel_f, kern[:, 0], precision=hp)
               + jnp.einsum('stl,lqhgxc->qhsgctx', sel_b, kern[:, 1], precision=hp))
    t_small = t_small.reshape(N_SG, 2, HALF_W, LANES)

    def cmul(p_re, p_im, w_re, w_im):
        return p_re * w_re - p_im * w_im, p_re * w_im + p_im * w_re

    dn_re, dn_im = apow(L - jnp.arange(L + 1, dtype=F32)[:, None, None, None])

    def powers(lo, d, descending):
        if descending:
            return dn_re[L + 1 - lo - L:L + 1 - lo, d], dn_im[L + 1 - lo - L:L + 1 - lo, d]
        return pw_re[lo:lo + L, d], pw_im[lo:lo + L, d]

    parts = []
    for d, flip in ((0, True), (1, False)):
        p_re, p_im = powers(0, d, flip)
        parts += list(cmul(p_re[:, :, None, :], p_im[:, :, None, :],
                           bb_re[d].transpose(0, 2, 1)[None], bb_im[d].transpose(0, 2, 1)[None]))
    wb_small = jnp.stack(parts, axis=3)
    wb_small = wb_small.reshape(L, N_SG, 2, HALF_GROUPS, SSM_GROUP, 4 * STATE_P).transpose(1, 2, 0, 3, 4, 5)
    wb_small = wb_small.reshape(N_SG, 2, HALF_W, 4 * STATE_P)

    parts = []
    for d, flip in ((0, False), (1, True)):
        p_re, p_im = powers(1, d, flip)
        z_re, z_im = cmul(p_re[:, :, None, :], p_im[:, :, None, :], cm_re[d][None], cm_im[d][None])
        parts += [z_re, -z_im]
    wc_small = jnp.stack(parts, axis=0).reshape(4, L, N_SG, 2, HALF_GROUPS, SSM_GROUP, STATE_P)
    wc_small = wc_small.transpose(2, 3, 0, 4, 6, 1, 5).reshape(N_SG, 2, 4 * HALF_STATE, LANES)

    lane = jnp.arange(LANES)
    col = jnp.arange(HALF_W)
    rep_lane = ((lane[:, None] // SSM_GROUP == col[None, :] // HALF_LANES)
                & (lane[:, None] % SSM_GROUP == col[None, :] % SSM_GROUP))
    st = jnp.arange(4 * STATE_P)
    scol = jnp.arange(4 * HALF_STATE)
    rep_state = ((st[:, None] // STATE_P == scol[None, :] // HALF_STATE)
                 & (st[:, None] % STATE_P == scol[None, :] % STATE_P))
    lane_group = (col % HALF_LANES) // SSM_GROUP
    state_group = (scol % HALF_STATE) // STATE_P

    def expand(small, rep, row_group, col_group):
        big = jnp.einsum('qhrk,kc->qhrc', small.astype(BF16), rep.astype(BF16), preferred_element_type=F32)
        return jnp.where(row_group[:, None] == col_group[None, :], big, 0.0).astype(BF16)

    t_mat = expand(t_small, rep_lane, lane_group, lane_group)
    wb_mat = expand(wb_small, rep_state, lane_group, state_group)
    wc_mat = expand(wc_small, rep_lane, state_group, lane_group)

    row = jnp.arange(SCAN_ROWS)
    sh = jnp.array([1, 2, 4])
    ones = jnp.ones((1, SCAN_ROWS), bool)
    keep = jnp.stack([jnp.concatenate([row[None, :] >= sh[:, None], ones]),
                      jnp.concatenate([row[None, :] <= SCAN_ROWS - 1 - sh[:, None], ones])])
    shifts = jnp.broadcast_to(sh[:, None], (3, SCAN_ROWS))
    expo = L * jnp.stack([jnp.concatenate([shifts, row[None, :] + 1]),
                          jnp.concatenate([shifts, SCAN_ROWS - row[None, :]])]).astype(F32)
    e = expo[:, :, :, None, None]
    mag = jnp.exp(ldt_re[:, None, None] * e)
    k5 = keep[:, :, :, None, None]
    cst = jnp.stack([jnp.where(k5, mag * jnp.cos(ldt_im[:, None, None] * e), 0.0),
                     jnp.where(k5, mag * jnp.sin(ldt_im[:, None, None] * e), 0.0)], axis=2)
    cst = cst.reshape(2, 4, 2, SCAN_ROWS, N_SG, SG_STATE).transpose(4, 0, 1, 2, 3, 5)

    dsk = jnp.tile(d_skip.astype(F32).reshape(N_SG, 2, 1, HALF_LANES), (1, 1, L, 1)).reshape(N_SG, 1, L * LANES)
    return t_mat, wb_mat, wc_mat, cst, dsk


def _ssm_kernel(u_ref, t_ref, wb_ref, wc_ref, cst_ref, d_ref, y_ref, stf_ref, stb_ref, *, n_chunks, mm_rows):
    n_mm = n_chunks // mm_rows
    tiles = mm_rows // SCAN_ROWS
    half = SG_STATE
    row_id = lax.broadcasted_iota(I32, (SCAN_ROWS, half), 0)
    zero = jnp.zeros((1, half), F32)

    def block(c):
        return pl.ds(pl.multiple_of(c * mm_rows, mm_rows), mm_rows)

    def scan_tile(st_ref, d, row0, carry):
        rows = pl.ds(pl.multiple_of(row0, SCAN_ROWS), SCAN_ROWS)
        hs = HALF_STATE
        xr = jnp.concatenate([st_ref[rows, 0:hs], st_ref[rows, 2 * hs:3 * hs]], axis=1)
        xi = jnp.concatenate([st_ref[rows, hs:2 * hs], st_ref[rows, 3 * hs:4 * hs]], axis=1)
        for si, sh in enumerate((1, 2, 4)):
            ar, ai = cst_ref[d, si, 0], cst_ref[d, si, 1]
            shift = sh if d == 0 else SCAN_ROWS - sh
            pr, pi = pltpu.roll(xr, shift, 0), pltpu.roll(xi, shift, 0)
            xr, xi = xr + (ar * pr - ai * pi), xi + (ar * pi + ai * pr)
        cr, ci = carry
        ar, ai = cst_ref[d, 3, 0], cst_ref[d, 3, 1]
        xr, xi = xr + (ar * cr - ai * ci), xi + (ar * ci + ai * cr)
        if d == 0:
            edge, shift, last = 0, 1, SCAN_ROWS - 1
        else:
            edge, shift, last = SCAN_ROWS - 1, SCAN_ROWS - 1, 0
        sr = jnp.where(row_id == edge, cr, pltpu.roll(xr, shift, 0))
        si_ = jnp.where(row_id == edge, ci, pltpu.roll(xi, shift, 0))
        st_ref[rows, 0:hs], st_ref[rows, 2 * hs:3 * hs] = sr[:, :hs], sr[:, hs:]
        st_ref[rows, hs:2 * hs], st_ref[rows, 3 * hs:4 * hs] = si_[:, :hs], si_[:, hs:]
        return xr[last:last + 1, :], xi[last:last + 1, :]

    halves = [(h, slice(h * HALF_W, (h + 1) * HALF_W)) for h in range(2)]

    def inject(st_ref, d, c, u=None):
        for h, cols in halves:
            uh = u_ref[block(c), cols] if u is None else u[:, cols]
            st_ref[block(c), cols] = _dot(uh, wb_ref[h, :, 2 * d * HALF_STATE:2 * (d + 1) * HALF_STATE])

    def eject(st_ref, d, c):
        for h, cols in halves:
            y_ref[block(c), cols] += _dot(st_ref[block(c), cols].astype(BF16),
                                          wc_ref[h, 2 * d * HALF_STATE:2 * (d + 1) * HALF_STATE, :])

    def inject_fwd(c, carry):
        inject(stf_ref, 0, c)
        return carry

    lax.fori_loop(0, n_mm, inject_fwd, 0)

    def forward(c, carry):
        u = u_ref[block(c), :]
        for h, cols in halves:
            y_ref[block(c), cols] = _dot(u[:, cols], t_ref[h]) + u[:, cols].astype(F32) * d_ref[:, cols]
        inject(stb_ref, 1, c, u)
        for k in range(tiles):
            carry = scan_tile(stf_ref, 0, c * mm_rows + k * SCAN_ROWS, carry)
        return carry

    lax.fori_loop(0, n_mm, forward, (zero, zero))

    def backward(i, carry):
        c = n_mm - 1 - i
        eject(stf_ref, 0, c)
        for k in reversed(range(tiles)):
            carry = scan_tile(stb_ref, 1, c * mm_rows + k * SCAN_ROWS, carry)
        return carry

    lax.fori_loop(0, n_mm, backward, (zero, zero))

    def eject_bwd(c, carry):
        eject(stb_ref, 1, c)
        return carry

    lax.fori_loop(0, n_mm, eject_bwd, 0)


def _ssm(u, tabs):
    t_mat, wb_mat, wc_mat, cst, dsk = tabs
    b, _, n_chunks, lw = u.shape
    mm_rows = min(256, n_chunks)
    assert n_chunks % mm_rows == 0 and n_chunks % SCAN_ROWS == 0
    one = pl.Buffered(1)
    wspec = lambda shape: pl.BlockSpec((None,) + shape, lambda q, i: (q,) + tuple(0 for _ in shape),
                                       pipeline_mode=one)
    return pl.pallas_call(
        functools.partial(_ssm_kernel, n_chunks=n_chunks, mm_rows=mm_rows),
        grid=(N_SG, b),
        in_specs=[pl.BlockSpec((None, None, n_chunks, lw), lambda q, i: (i, q, 0, 0)),
                  wspec((2, HALF_W, HALF_W)), wspec((2, HALF_W, 4 * HALF_STATE)), wspec((2, 4 * HALF_STATE, HALF_W)),
                  wspec((2, 4, 2, SCAN_ROWS, SG_STATE)), wspec((1, lw))],
        out_specs=pl.BlockSpec((None, None, n_chunks, lw), lambda q, i: (i, q, 0, 0)),
        out_shape=jax.ShapeDtypeStruct((b, N_SG, n_chunks, lw), F32),
        scratch_shapes=[pltpu.VMEM((n_chunks, 2 * SG_STATE), F32)] * 2,
        compiler_params=_cparams(("arbitrary", "arbitrary")),
        name="ssm",
    )(u, t_mat, wb_mat, wc_mat, cst, dsk)


def _merge_kernel(x_ref, attn_ref, y_ref, ga_ref, gs_ref, g1_ref, sc_ref, sh_ref, nf_ref,
                  wab_ref, wglu_ref, wout_ref, wr_ref, x1_ref, h2_ref, lg_ref, gel_scr):
    sub = x_ref.shape[0] // MERGE_SPLIT
    crows = sub // SSM_L
    blocks = [(k, slice(k * sub, (k + 1) * sub)) for k in range(MERGE_SPLIT)]

    ab = [_dot(attn_ref[rows, :], wab_ref[...]) for _, rows in blocks]

    low = lax.broadcasted_iota(I32, (crows, LANES), 1) < HALF_LANES
    gel = []
    for k, rows in blocks:
        for sg in range(N_SG):
            for sp in range(SSM_L // 2):
                t0 = y_ref[sg, k * crows:(k + 1) * crows, sp * LANES:(sp + 1) * LANES]
                t1 = y_ref[sg, k * crows:(k + 1) * crows, HALF_W + sp * LANES:HALF_W + (sp + 1) * LANES]
                pair = (jnp.where(low, t0, pltpu.roll(t1, HALF_LANES, 1)),
                        jnp.where(low, pltpu.roll(t0, HALF_LANES, 1), t1))
                for s, y in zip((2 * sp, 2 * sp + 1), pair):
                    gel_scr[sg, k * sub + s * crows:k * sub + (s + 1) * crows, :] = (
                        0.5 * y * (1.0 + jnp.tanh(math.sqrt(2.0 / math.pi) * (y + 0.044715 * (y * y * y)))))
        gel.append(jnp.concatenate(
            [jnp.concatenate([gel_scr[sg, pl.ds(k * sub + j, SSM_L, stride=crows), :] for j in range(crows)], axis=0)
             for sg in range(N_SG)], axis=1).astype(BF16))
    glu = [_dot(g, wglu_ref[...]) for g in gel]
    merged = []
    for (k, rows), ab_k, glu_k in zip(blocks, ab, glu):
        sb = glu_k[:, :D_MODEL] * _sigmoid(glu_k[:, D_MODEL:])
        merged.append((ga_ref[rows, :].astype(F32) * ab_k + gs_ref[rows, :].astype(F32) * sb).astype(BF16))
    mixed = [_dot(m, wout_ref[...]) for m in merged]
    h2s = []
    for (k, rows), mix in zip(blocks, mixed):
        x1 = x_ref[rows, :] + g1_ref[...] * mix
        x1_ref[rows, :] = x1
        ms = jnp.mean(x1 * x1, axis=-1, keepdims=True)
        h2 = x1 * lax.rsqrt(ms + EPS) * nf_ref[...] * (1.0 + sc_ref[...]) + sh_ref[...]
        h2_ref[rows, :D_MODEL] = h2.astype(BF16)
        h2s.append(h2)
    for (k, rows), h2 in zip(blocks, h2s):
        lg = _dot3(h2, wr_ref[...])
        lg_ref[rows, :] = lg
        valid = lax.broadcasted_iota(I32, lg.shape, 1) < N_EXPERTS
        m = jnp.max(jnp.where(valid, lg, -jnp.inf), axis=-1, keepdims=True)
        ex = jnp.where(valid, jnp.exp(lg - m), 0.0)
        aff = ex / jnp.sum(ex, axis=-1, keepdims=True)
        hi = aff.astype(BF16).astype(F32)
        mid = (aff - hi).astype(BF16).astype(F32)
        lo = (aff - hi - mid).astype(BF16).astype(F32)
        parts = hi + pltpu.roll(mid, N_EXPERTS, 1) + pltpu.roll(lo, 2 * N_EXPERTS, 1)
        h2_ref[rows, D_MODEL:] = parts.astype(BF16)


def _merge(x, attn, y4, ga, gs, g1, sc2, sh2, norm_ffn, wab, wglu, wout, wr_pad, bt):
    b, n, d = x.shape
    tok = lambda w: pl.BlockSpec((None, bt, w), lambda i, j: (i, j, 0))
    mod = pl.BlockSpec((None, 1, d), lambda i, j: (i, 0, 0))
    full = lambda shape: pl.BlockSpec(shape, lambda i, j: tuple(0 for _ in shape))
    return pl.pallas_call(
        _merge_kernel,
        grid=(b, n // bt),
        in_specs=[tok(d), tok(D_ATTN),
                  pl.BlockSpec((None, N_SG, bt // SSM_L, SSM_L * LANES), lambda i, j: (i, 0, j, 0)),
                  tok(d), tok(d), mod, mod, mod, full((1, d)),
                  full((D_ATTN, d)), full((D_SSM, 2 * d)), full((d, d)), full((d, LANES))],
        out_specs=[tok(d), tok(d + LANES), tok(LANES)],
        out_shape=[jax.ShapeDtypeStruct((b, n, d), F32), jax.ShapeDtypeStruct((b, n, d + LANES), BF16),
                   jax.ShapeDtypeStruct((b, n, LANES), F32)],
        scratch_shapes=[pltpu.VMEM((N_SG, bt, LANES), F32)],
        compiler_params=_cparams(("parallel", "parallel")),
        name="merge",
    )(x, attn, y4, ga, gs, g1, sc2, sh2, norm_ffn, wab, wglu, wout, wr_pad)


def _route_kernel(lg_ref, tri_ref, slot_ref, cnt_ref, aff_ref, *, cap):
    lg = lg_ref[...]
    m = jnp.max(lg, axis=1, keepdims=True)
    e = jnp.exp(lg - m)
    aff = e / jnp.sum(e, axis=1, keepdims=True)
    aff_ref[...] = aff

    def count(mask):
        c = jnp.sum(jnp.where(mask, 1.0, 0.0), axis=0, keepdims=True)
        return jnp.sum(c, axis=2, keepdims=True)

    def as_float(bits):
        return lax.bitcast_convert_type(bits, F32)

    def bit_step(i, thr):
        cand = thr | (jnp.int32(1) << (30 - i))
        return jnp.where(count(aff >= as_float(cand)) >= cap, cand, thr)

    thr3 = as_float(lax.fori_loop(0, 31, bit_step, jnp.zeros((1, N_EXPERTS, 1), I32)))
    need = (cap - count(aff > thr3))[0]
    thr = thr3[0]
    n_chunks = lg.shape[0]
    carry0 = jnp.zeros((N_EXPERTS, 1), F32)

    def prefix(flag, carry):
        f = jnp.where(flag, 1.0, 0.0)
        inc = _dot(f.astype(BF16), tri_ref[...]) + carry
        return inc, inc - f

    def tie_body(c, carry):
        a = aff_ref[c]
        tie = a == thr
        inc, rank = prefix(tie, carry)
        sel = (a > thr) | (tie & (rank < need))
        slot_ref[c] = jnp.where(sel, 1, 0).astype(I32)
        return inc[:, CUM_W - 1:CUM_W]

    lax.fori_loop(0, n_chunks, tie_body, carry0)

    def slot_body(c, carry):
        sel = slot_ref[c] > 0
        inc, excl = prefix(sel, carry)
        slot_ref[c] = jnp.where(sel, excl, -1.0).astype(I32)
        cnt_ref[c] = excl.astype(I32)
        return inc[:, CUM_W - 1:CUM_W]

    lax.fori_loop(0, n_chunks, slot_body, carry0)


def _route(logits_c, tri, cap):
    nc = logits_c.shape[0]
    shp = (nc, N_EXPERTS, CUM_W)
    return pl.pallas_call(
        functools.partial(_route_kernel, cap=cap),
        grid=(1,),
        in_specs=[pl.BlockSpec(shp, lambda i: (0, 0, 0)), pl.BlockSpec((CUM_W, CUM_W), lambda i: (0, 0))],
        out_specs=[pl.BlockSpec(shp, lambda i: (0, 0, 0))] * 2,
        out_shape=[jax.ShapeDtypeStruct(shp, I32)] * 2,
        scratch_shapes=[pltpu.VMEM(shp, F32)],
        compiler_params=_cparams(("arbitrary",)),
        name="route",
    )(logits_c, tri)


def _gather_kernel(offs_ref, h_ref, slot_ref, xe_ref, *, nb, cap, win):
    eg = pl.program_id(0)
    blk = pl.program_id(1)

    @pl.when(blk == 0)
    def _():
        xe_ref[...] = jnp.zeros_like(xe_ref)

    riota = lax.broadcasted_iota(I32, (win, 1), 0)

    def window(ee, w):
        off = offs_ref[(eg * GATHER_EXPERTS + ee) * (nb + 1) + blk]
        nominal = (off // BF16_ROWS) * BF16_ROWS + w * win
        start = pl.multiple_of(jnp.minimum(nominal, cap - win), BF16_ROWS)
        slot = slot_ref[ee]
        hit = (slot == riota + start) & (slot >= nominal)
        return jnp.where(hit, 1.0, 0.0).astype(BF16), start

    firsts = [window(ee, 0) for ee in range(GATHER_EXPERTS)]
    picked = _dot(jnp.concatenate([oh for oh, _ in firsts], axis=0), h_ref[...]).astype(BF16)
    for ee, (_, start) in enumerate(firsts):
        xe_ref[ee, pl.ds(start, win), :] += picked[ee * win:(ee + 1) * win]

    for ee in range(GATHER_EXPERTS):
        off = offs_ref[(eg * GATHER_EXPERTS + ee) * (nb + 1) + blk]
        end = offs_ref[(eg * GATHER_EXPERTS + ee) * (nb + 1) + blk + 1]
        n_win = (end - (off // BF16_ROWS) * BF16_ROWS + win - 1) // win

        def extra(w, carry, ee=ee):
            onehot, start = window(ee, w)
            xe_ref[ee, pl.ds(start, win), :] += _dot(onehot, h_ref[...]).astype(BF16)
            return carry

        lax.fori_loop(1, n_win, extra, 0)


def _gather(offs_flat, h2, slot3, cap, bt):
    t, d = h2.shape
    nb = t // bt
    win = min(GATHER_WIN, cap)
    ge = GATHER_EXPERTS
    gs = pltpu.PrefetchScalarGridSpec(
        num_scalar_prefetch=1,
        grid=(N_EXPERTS // ge, nb),
        in_specs=[pl.BlockSpec((bt, d), lambda e, j, offs: (j, 0)),
                  pl.BlockSpec((ge, 1, bt), lambda e, j, offs: (e, 0, j))],
        out_specs=pl.BlockSpec((ge, cap, d), lambda e, j, offs: (e, 0, 0), pipeline_mode=pl.Buffered(1)),
    )
    return pl.pallas_call(
        functools.partial(_gather_kernel, nb=nb, cap=cap, win=win),
        grid_spec=gs,
        out_shape=jax.ShapeDtypeStruct((N_EXPERTS, cap, d), BF16),
        compiler_params=_cparams(("arbitrary", "arbitrary")),
        name="gather",
    )(offs_flat, h2, slot3)


def _ffn_kernel(x_ref, wg_ref, wu_ref, wd_ref, y_ref, *, fchunk):
    x = x_ref[:, :D_MODEL]
    acc = jnp.zeros((x.shape[0], D_MODEL), F32)
    for f in range(D_EXPERT // fchunk):
        fs = slice(f * fchunk, (f + 1) * fchunk)
        a = _dot(x, wg_ref[:, fs])
        u = _dot(x, wu_ref[:, fs])
        hmid = (a * _sigmoid(a) * u).astype(BF16)
        acc = acc + _dot(hmid, wd_ref[fs, :])
    parts = x_ref[:, D_MODEL:].astype(F32)
    lane = lax.broadcasted_iota(I32, parts.shape, 1)
    mine = (lane % N_EXPERTS == pl.program_id(0)) & (lane < 3 * N_EXPERTS)
    gate = jnp.sum(jnp.where(mine, parts, 0.0), axis=-1, keepdims=True)
    y_ref[...] = (gate * acc).astype(BF16)


def _ffn(xe, wg, wu, wd, tm):
    e, cap, dx = xe.shape
    d = D_MODEL
    return pl.pallas_call(
        functools.partial(_ffn_kernel, fchunk=512),
        grid=(e, cap // tm),
        in_specs=[pl.BlockSpec((None, tm, dx), lambda i, j: (i, j, 0)),
                  pl.BlockSpec((None, d, D_EXPERT), lambda i, j: (i, 0, 0)),
                  pl.BlockSpec((None, d, D_EXPERT), lambda i, j: (i, 0, 0)),
                  pl.BlockSpec((None, D_EXPERT, d), lambda i, j: (i, 0, 0))],
        out_specs=pl.BlockSpec((None, tm, d), lambda i, j: (i, j, 0)),
        out_shape=jax.ShapeDtypeStruct((e, cap, d), BF16),
        compiler_params=_cparams(("parallel", "parallel")),
        name="ffn",
    )(xe, wg, wu, wd)


def _window_copy(ye_hbm, buf, sem, e, start, win):
    return pltpu.make_async_copy(ye_hbm.at[e, pl.ds(start, win), :], buf, sem)


def _combine_kernel(offs_ref, x1_ref, slot_ref, g2_ref, ye_hbm, o_ref, ybuf, xbuf, lhs, sems, xsem,
                    *, nb, cap, win):
    blk = pl.program_id(0)

    def first_window(b_, e):
        off = offs_ref[e * (nb + 1) + b_]
        a0 = (off // BF16_ROWS) * BF16_ROWS
        return a0, pl.multiple_of(jnp.minimum(a0, cap - win), BF16_ROWS)

    def window_copies(b_, half):
        return [_window_copy(ye_hbm, ybuf.at[half, pl.ds(e * win, win)], sems.at[half, e], e,
                             first_window(b_, e)[1], win) for e in range(N_EXPERTS)]

    @pl.when(blk == 0)
    def _():
        for cp in window_copies(0, 0):
            cp.start()

    @pl.when(blk + 1 < nb)
    def _():
        for cp in window_copies(blk + 1, (blk + 1) % 2):
            cp.start()

    starts = [first_window(blk, e) for e in range(N_EXPERTS)]
    liota = lax.broadcasted_iota(I32, (1, win), 1)
    slots = slot_ref[...]
    for e in range(N_EXPERTS):
        hit = slots[:, e:e + 1] == liota + starts[e][1]
        lhs[:, e * win:(e + 1) * win] = jnp.where(hit, 1.0, 0.0).astype(BF16)
    for cp in window_copies(blk, blk % 2):
        cp.wait()
    o_ref[...] = x1_ref[...] + g2_ref[...] * _dot(lhs[...], ybuf[blk % 2])

    for e in range(N_EXPERTS):
        a0 = starts[e][0]
        end = offs_ref[e * (nb + 1) + blk + 1]
        n_win = (end - a0 + win - 1) // win

        def extra(w, carry, e=e, a0=a0):
            nominal = a0 + w * win
            st = pl.multiple_of(jnp.minimum(nominal, cap - win), BF16_ROWS)
            cp = _window_copy(ye_hbm, xbuf, xsem, e, st, win)
            cp.start()
            cp.wait()
            scol = slot_ref[:, e:e + 1]
            hit = (scol == liota + st) & (scol >= nominal)
            o_ref[...] += g2_ref[...] * _dot(jnp.where(hit, 1.0, 0.0).astype(BF16), xbuf[...])
            return carry

        lax.fori_loop(1, n_win, extra, 0)


def _combine(offs_flat, x1, slot_t, g2, ye, n_per_batch, bt):
    t, d = x1.shape
    nb = t // bt
    cap = ye.shape[1]
    win = min(SLOT_WIN, cap)
    per = n_per_batch // bt
    gs = pltpu.PrefetchScalarGridSpec(
        num_scalar_prefetch=1,
        grid=(nb,),
        in_specs=[pl.BlockSpec((bt, d), lambda j, offs: (j, 0)),
                  pl.BlockSpec((bt, N_EXPERTS), lambda j, offs: (j, 0)),
                  pl.BlockSpec((None, 1, d), lambda j, offs: (j // per, 0, 0)),
                  pl.BlockSpec(memory_space=pl.ANY)],
        out_specs=pl.BlockSpec((bt, d), lambda j, offs: (j, 0)),
        scratch_shapes=[pltpu.VMEM((2, N_EXPERTS * win, d), BF16), pltpu.VMEM((win, d), BF16),
                        pltpu.VMEM((bt, N_EXPERTS * win), BF16),
                        pltpu.SemaphoreType.DMA((2, N_EXPERTS)), pltpu.SemaphoreType.DMA(())],
    )
    return pl.pallas_call(
        functools.partial(_combine_kernel, nb=nb, cap=cap, win=win),
        grid_spec=gs,
        out_shape=jax.ShapeDtypeStruct((t, d), F32),
        compiler_params=_cparams(("arbitrary",)),
        name="combine",
    )(offs_flat, x1, slot_t, g2, ye)


def _prep_weights(w_in, q_norm, k_norm, rpb, ssm_params, w_glu, w_attn_br, w_out, w_router,
                  w_exp_gate, w_exp_up, w_exp_down):
    head = jnp.arange(D_ATTN) // HEAD_DIM
    return dict(
        w_in=w_in.astype(BF16),
        qg=jnp.tile(q_norm.astype(F32), N_HEADS).reshape(1, D_ATTN),
        kg=jnp.tile(k_norm.astype(F32), N_HEADS).reshape(1, D_ATTN),
        ones_bd=(head[:, None] == head[None, :]).astype(BF16),
        bias_tab=_attn_bias_table(rpb),
        ssm=_ssm_tables(*ssm_params),
        wglu=w_glu.astype(BF16), wab=w_attn_br.astype(BF16), wout=w_out.astype(BF16),
        wr=jnp.pad(w_router.astype(F32), ((0, 0), (0, LANES - N_EXPERTS))),
        wg=w_exp_gate.astype(BF16), wu=w_exp_up.astype(BF16), wd=w_exp_down.astype(BF16),
        tri=(jnp.arange(CUM_W)[:, None] <= jnp.arange(CUM_W)[None, :]).astype(BF16),
    )


def _token_block(n, want):
    bt = min(want, n)
    assert n % bt == 0
    return bt


def _encoder_layer(x, c, w_ada, b_ada, norm_mix, norm_ffn, wts):
    b, n, d = x.shape
    t = b * n
    cap = EC_CAPACITY * t // N_EXPERTS

    c_pad = jnp.pad(c.astype(F32), ((0, (-b) % SUBLANES), (0, 0)))
    mod = _ada(c_pad, w_ada, b_ada)[:b]
    sh1, sc1, g1, sh2, sc2, g2 = [m.reshape(b, 1, d) for m in jnp.split(mod, 6, axis=-1)]

    bt = _token_block(n, 512)
    q, k, v, u4, ga, gs = _inproj(x, sc1, sh1, norm_mix.reshape(1, d), wts["w_in"], wts["qg"], wts["kg"],
                                  wts["ones_bd"], bt)
    attn = _attention(q, k, v, wts["bias_tab"])
    y4 = _ssm(u4, wts["ssm"])
    x1, h2, logits = _merge(x, attn, y4, ga, gs, g1, sc2, sh2, norm_ffn.reshape(1, d),
                            wts["wab"], wts["wglu"], wts["wout"], wts["wr"], bt)

    lg = logits.reshape(t, LANES)[:, :N_EXPERTS]
    lg_c = lg.reshape(t // CUM_W, CUM_W, N_EXPERTS).transpose(0, 2, 1)
    slot_c, cnt_c = _route(lg_c, wts["tri"], cap)
    slot_et = slot_c.transpose(1, 0, 2).reshape(N_EXPERTS, t)
    slot_te = slot_et.T

    bt2 = _token_block(t, 512)
    cnt_at_block = cnt_c.transpose(1, 0, 2).reshape(N_EXPERTS, t)[:, ::bt2]
    offs = jnp.concatenate([cnt_at_block, jnp.full((N_EXPERTS, 1), cap, I32)], axis=-1)
    offs_flat = offs.reshape(-1).astype(I32)

    xe = _gather(offs_flat, h2.reshape(t, d + LANES), slot_et.reshape(N_EXPERTS, 1, t), cap, bt2)
    ye = _ffn(xe, wts["wg"], wts["wu"], wts["wd"], _token_block(cap, 1024))
    out = _combine(offs_flat, x1.reshape(t, d), slot_te, g2, ye, n, bt2)
    return out.reshape(b, n, d)


def kernel(x_prompt, x_sample, c_prompt, c_sample, w_ada, b_ada, norm_mix, norm_ffn, w_in, q_norm, k_norm, rpb,
           ssm_a_re, ssm_a_im, ssm_log_dt, ssm_b_re, ssm_b_im, ssm_c_re, ssm_c_im, ssm_d, w_glu, w_attn_br,
           w_out, w_router, w_exp_gate, w_exp_up, w_exp_down):
    y_prompt, y_sample = x_prompt, x_sample
    for layer in range(w_ada.shape[0]):
        ssm_params = tuple(p[layer] for p in (ssm_a_re, ssm_a_im, ssm_log_dt, ssm_b_re, ssm_b_im,
                                              ssm_c_re, ssm_c_im, ssm_d))
        wts = _prep_weights(w_in[layer], q_norm[layer], k_norm[layer], rpb[layer], ssm_params, w_glu[layer],
                            w_attn_br[layer], w_out[layer], w_router[layer], w_exp_gate[layer],
                            w_exp_up[layer], w_exp_down[layer])
        y_prompt = _encoder_layer(y_prompt, c_prompt, w_ada[layer], b_ada[layer], norm_mix[layer],
                                  norm_ffn[layer], wts)
        y_sample = _encoder_layer(y_sample, c_sample, w_ada[layer], b_ada[layer], norm_mix[layer],
                                  norm_ffn[layer], wts)
    return (y_prompt, y_sample)
```

```python
import functools
import math

import jax
import jax.numpy as jnp
from jax import lax
from jax.experimental import pallas as pl
from jax.experimental.pallas import tpu as pltpu

F32 = jnp.float32
BF16 = jnp.bfloat16
I32 = jnp.int32

D_MODEL = 1024
GRID_W = 64
N_HEADS = 8
HEAD_DIM = 64
D_ATTN = N_HEADS * HEAD_DIM
WIN_R = 8
WIN_C = 16
SSM_GROUP = 16
D_SSM = 512
N_GROUPS = D_SSM // SSM_GROUP
STATE_P = 64
D_IN = 3 * D_ATTN + D_SSM + 2 * D_MODEL
N_EXPERTS = 16
EC_CAPACITY = 2
D_EXPERT = 2048
EPS = 1e-6
NEG_INF = -1e9

LANES = 128
SUBLANES = 8
BF16_ROWS = 16
VMEM_LIMIT = 56 * 1024 * 1024

SSM_L = 8
SG_GROUPS = LANES // SSM_GROUP
N_SG = N_GROUPS // SG_GROUPS
SG_STATE = SG_GROUPS * STATE_P
HALF_GROUPS = SG_GROUPS // 2
HALF_LANES = LANES // 2
HALF_W = SSM_L * HALF_LANES
HALF_STATE = HALF_GROUPS * STATE_P
SCAN_ROWS = SUBLANES

ATTN_ROWS = 8
ATTN_UNROLL = 4
MERGE_SPLIT = 4
INPROJ_SPLIT = 1
SLOT_WIN = 128
GATHER_WIN = 96
GATHER_BLOCKS = 4
GATHER_EXPERTS = 4
CUM_W = 256


def _cparams(sem):
    return pltpu.CompilerParams(dimension_semantics=sem, vmem_limit_bytes=VMEM_LIMIT)


def _split_bf16(a):
    hi = a.astype(BF16)
    lo = (a - hi.astype(F32)).astype(BF16)
    return hi, lo


def _dot(a, b):
    return jnp.dot(a, b, preferred_element_type=F32)


def _dot3(a, b):
    ah, al = _split_bf16(a)
    bh, bl = _split_bf16(b)
    return _dot(ah, bh) + (_dot(ah, bl) + _dot(al, bh))


def _sigmoid(z):
    return 1.0 / (1.0 + jnp.exp(-z))


def _ada_kernel(c_ref, w_ref, b_ref, o_ref):
    c = c_ref[...]
    s = c * _sigmoid(c)
    o_ref[...] = _dot3(s, w_ref[...]) + b_ref[...]


def _ada(c_pad, w_ada, b_ada):
    rows = c_pad.shape[0]
    n_out = w_ada.shape[1]
    return pl.pallas_call(
        _ada_kernel,
        grid=(n_out // D_MODEL,),
        in_specs=[pl.BlockSpec((rows, D_MODEL), lambda j: (0, 0)),
                  pl.BlockSpec((D_MODEL, D_MODEL), lambda j: (0, j)),
                  pl.BlockSpec((1, D_MODEL), lambda j: (0, j))],
        out_specs=pl.BlockSpec((rows, D_MODEL), lambda j: (0, j)),
        out_shape=jax.ShapeDtypeStruct((rows, n_out), F32),
        compiler_params=_cparams(("arbitrary",)),
        name="ada",
    )(c_pad, w_ada, b_ada.reshape(1, n_out))


def _inproj_kernel(x_ref, sc_ref, sh_ref, nm_ref, w_ref, qg_ref, kg_ref, ones_ref,
                   q_ref, k_ref, v_ref, u_ref, ga_ref, gs_ref, h_scr):
    def head_norm(z, gain):
        ssum = _dot((z * z).astype(BF16), ones_ref[...])
        return z * lax.rsqrt(ssum * (1.0 / HEAD_DIM) + EPS) * gain

    sub = x_ref.shape[0] // INPROJ_SPLIT
    crows = sub // SSM_L
    for kb in range(INPROJ_SPLIT):
        rows = slice(kb * sub, (kb + 1) * sub)
        x = x_ref[rows, :]
        ms = jnp.mean(x * x, axis=-1, keepdims=True)
        xn = x * lax.rsqrt(ms + EPS) * nm_ref[...]
        hf = xn * (1.0 + sc_ref[...]) + sh_ref[...]
        for c in range(D_MODEL // LANES):
            h_scr[c, rows, :] = hf[:, c * LANES:(c + 1) * LANES]
        h = hf.astype(BF16)

        def proj(lo, hi, h=h):
            return _dot(h, w_ref[:, lo:hi])

        q = head_norm(proj(0, D_ATTN), qg_ref[...]) * (HEAD_DIM ** -0.5)
        q_ref[rows, :] = q.astype(BF16)
        k = head_norm(proj(D_ATTN, 2 * D_ATTN), kg_ref[...])
        k_ref[rows, :] = k.astype(BF16)
        v_ref[rows, :] = proj(2 * D_ATTN, 3 * D_ATTN).astype(BF16)
        hp = jnp.concatenate(
            [jnp.concatenate([h_scr[c, pl.ds(kb * sub + s, crows, stride=SSM_L), :]
                              for c in range(D_MODEL // LANES)], axis=1)
             for s in range(SSM_L)], axis=0)
        u = _dot(hp.astype(BF16), w_ref[:, 3 * D_ATTN:3 * D_ATTN + D_SSM])
        low = lax.broadcasted_iota(I32, (crows, LANES), 1) < HALF_LANES
        out_rows = slice(kb * crows, (kb + 1) * crows)
        for sg in range(N_SG):
            for sp in range(SSM_L // 2):
                a = u[(2 * sp) * crows:(2 * sp + 1) * crows, sg * LANES:(sg + 1) * LANES]
                b = u[(2 * sp + 1) * crows:(2 * sp + 2) * crows, sg * LANES:(sg + 1) * LANES]
                u_ref[sg, out_rows, sp * LANES:(sp + 1) * LANES] = (
                    jnp.where(low, a, pltpu.roll(b, HALF_LANES, 1)).astype(BF16))
                u_ref[sg, out_rows, HALF_W + sp * LANES:HALF_W + (sp + 1) * LANES] = (
                    jnp.where(low, pltpu.roll(a, HALF_LANES, 1), b).astype(BF16))
        o = 3 * D_ATTN + D_SSM
        ga_ref[rows, :] = _sigmoid(proj(o, o + D_MODEL)).astype(BF16)
        gs_ref[rows, :] = _sigmoid(proj(o + D_MODEL, o + 2 * D_MODEL)).astype(BF16)


def _inproj(x, sc1, sh1, norm_mix, w_in_bf, qg, kg, ones_bd, bt):
    b, n, d = x.shape
    tok = lambda w: pl.BlockSpec((None, bt, w), lambda i, j: (i, j, 0))
    mod = pl.BlockSpec((None, 1, d), lambda i, j: (i, 0, 0))
    full = lambda shape: pl.BlockSpec(shape, lambda i, j: tuple(0 for _ in shape))
    return pl.pallas_call(
        _inproj_kernel,
        grid=(b, n // bt),
        in_specs=[tok(d), mod, mod, full((1, d)), full((d, D_IN)),
                  full((1, D_ATTN)), full((1, D_ATTN)), full((D_ATTN, D_ATTN))],
        out_specs=[tok(D_ATTN), tok(D_ATTN), tok(D_ATTN),
                   pl.BlockSpec((None, N_SG, bt // SSM_L, SSM_L * LANES), lambda i, j: (i, 0, j, 0)),
                   tok(d), tok(d)],
        out_shape=[jax.ShapeDtypeStruct((b, n, D_ATTN), BF16)] * 3
        + [jax.ShapeDtypeStruct((b, N_SG, n // SSM_L, SSM_L * LANES), BF16)]
        + [jax.ShapeDtypeStruct((b, n, d), BF16)] * 2,
        scratch_shapes=[pltpu.VMEM((d // LANES, bt, LANES), F32)],
        compiler_params=_cparams(("parallel", "parallel")),
        name="inproj",
    )(x, sc1, sh1, norm_mix, w_in_bf, qg, kg, ones_bd)


def _attn_kernel(q_ref, kp_ref, kc_ref, kn_ref, vp_ref, vc_ref, vn_ref, bias_ref, o_ref,
                 kwin, vwin, *, rows):
    r0 = pl.program_id(1) * ATTN_ROWS
    rw = ATTN_ROWS * GRID_W
    for t, (kr, vr) in enumerate(((kp_ref, vp_ref), (kc_ref, vc_ref), (kn_ref, vn_ref))):
        kwin[t * rw:(t + 1) * rw, :] = kr[...]
        vwin[t * rw:(t + 1) * rw, :] = vr[...]
    even = lax.broadcasted_iota(I32, (GRID_W, LANES), 1) < HEAD_DIM
    nkeys = WIN_R * GRID_W
    pairs = N_HEADS // 2

    def rows_body(it, carry):
        units = []
        for sub in range(ATTN_UNROLL):
            i = it * ATTN_UNROLL + sub
            r = r0 + i
            rs = jnp.clip(r - WIN_R // 2, 0, rows - WIN_R)
            variant = r - rs
            koff = pl.multiple_of((rs - r0 + ATTN_ROWS) * GRID_W, GRID_W)
            qoff = pl.multiple_of(i * GRID_W, GRID_W)
            for hp in range(pairs):
                ls = slice(hp * LANES, (hp + 1) * LANES)
                qp = q_ref[pl.ds(qoff, GRID_W), ls]
                zero = jnp.zeros_like(qp)
                q2 = jnp.concatenate([jnp.where(even, qp, zero), jnp.where(even, zero, qp)], axis=0)
                kp = kwin[pl.ds(koff, nkeys), ls]
                s = lax.dot_general(q2, kp, (((1,), (1,)), ((), ())), preferred_element_type=F32)
                units.append((qoff, koff, ls, s + bias_ref[variant, hp]))
        probs = []
        for qoff, koff, ls, s in units:
            p = jnp.exp(s - jnp.max(s, axis=-1, keepdims=True))
            probs.append((p.astype(BF16), jnp.sum(p, axis=-1, keepdims=True)))
        for (qoff, koff, ls, _), (p, l) in zip(units, probs):
            o2 = _dot(p, vwin[pl.ds(koff, nkeys), ls]) / l
            o_ref[pl.ds(qoff, GRID_W), ls] = jnp.where(even, o2[:GRID_W], o2[GRID_W:]).astype(BF16)
        return carry

    lax.fori_loop(0, ATTN_ROWS // ATTN_UNROLL, rows_body, 0)


def _attention(q, k, v, bias_tab):
    b, n, _ = q.shape
    rows = n // GRID_W
    assert rows % ATTN_ROWS == 0 and rows >= WIN_R
    nblk = rows // ATTN_ROWS
    rw = ATTN_ROWS * GRID_W
    cur = pl.BlockSpec((None, rw, D_ATTN), lambda i, j: (i, j, 0))
    prv = pl.BlockSpec((None, rw, D_ATTN), lambda i, j: (i, jnp.maximum(j - 1, 0), 0))
    nxt = pl.BlockSpec((None, rw, D_ATTN), lambda i, j: (i, jnp.minimum(j + 1, nblk - 1), 0))
    return pl.pallas_call(
        functools.partial(_attn_kernel, rows=rows),
        grid=(b, nblk),
        in_specs=[cur, prv, cur, nxt, prv, cur, nxt,
                  pl.BlockSpec(bias_tab.shape, lambda i, j: (0, 0, 0, 0))],
        out_specs=cur,
        out_shape=jax.ShapeDtypeStruct((b, n, D_ATTN), BF16),
        scratch_shapes=[pltpu.VMEM((3 * rw, D_ATTN), BF16), pltpu.VMEM((3 * rw, D_ATTN), BF16)],
        compiler_params=_cparams(("parallel", "parallel")),
        name="attn",
    )(q, k, k, k, v, v, v, bias_tab)


def _attn_bias_table(rpb):
    var = jnp.arange(WIN_R)
    a = jnp.arange(WIN_R)
    j = jnp.arange(GRID_W)
    c_start = jnp.clip(j - WIN_C // 2, 0, GRID_W - WIN_C)
    col_ok = (j[None, :] >= c_start[:, None]) & (j[None, :] < c_start[:, None] + WIN_C)
    col_off = jnp.clip(j[None, :] - j[:, None], -(WIN_C - 1), WIN_C - 1) + WIN_C - 1
    row_off = a[None, :] - var[:, None] + WIN_R - 1
    hp = lax.Precision.HIGHEST
    row_sel = (row_off[:, :, None] == jnp.arange(2 * WIN_R - 1)).astype(F32)
    col_sel = (col_off[:, :, None] == jnp.arange(2 * WIN_C - 1)).astype(F32)
    tab = jnp.einsum('hrc,var->hvac', rpb.astype(F32), row_sel, precision=hp)
    tab = jnp.einsum('hvac,jkc->vhjak', tab, col_sel, precision=hp)
    tab = jnp.where(col_ok[None, None, :, None, :], tab, NEG_INF)
    return tab.reshape(WIN_R, N_HEADS // 2, 2 * GRID_W, WIN_R * GRID_W)


def _ssm_tables(a_re, a_im, log_dt, b_re, b_im, c_re, c_im, d_skip):
    L = SSM_L
    hp = lax.Precision.HIGHEST
    lam_re, lam_im = a_re.astype(F32), a_im.astype(F32)
    dt = jnp.exp(log_dt.astype(F32))[..., None]
    ldt_re, ldt_im = lam_re * dt, lam_im * dt

    def apow(kk):
        mag = jnp.exp(ldt_re * kk)
        return mag * jnp.cos(ldt_im * kk), mag * jnp.sin(ldt_im * kk)

    pw_re, pw_im = apow(jnp.arange(L + 1, dtype=F32)[:, None, None, None])
    den = lam_re * lam_re + lam_im * lam_im
    co_re = ((pw_re[1] - 1.0) * lam_re + pw_im[1] * lam_im) / den
    co_im = (pw_im[1] * lam_re - (pw_re[1] - 1.0) * lam_im) / den
    bm_re, bm_im = b_re.astype(F32), b_im.astype(F32)
    bb_re = co_re[..., None] * bm_re - co_im[..., None] * bm_im
    bb_im = co_re[..., None] * bm_im + co_im[..., None] * bm_re
    cm_re, cm_im = c_re.astype(F32), c_im.astype(F32)

    e_re = cm_re[None] * pw_re[:L, :, :, None, :] - cm_im[None] * pw_im[:L, :, :, None, :]
    e_im = cm_re[None] * pw_im[:L, :, :, None, :] + cm_im[None] * pw_re[:L, :, :, None, :]
    kern = (jnp.einsum('ldgxp,dgpc->ldgxc', e_re, bb_re, precision=hp)
            - jnp.einsum('ldgxp,dgpc->ldgxc', e_im, bb_im, precision=hp))
    kern = kern.reshape(L, 2, N_SG, 2, HALF_GROUPS, SSM_GROUP, SSM_GROUP)
    s_idx = jnp.arange(L)[:, None]
    t_idx = jnp.arange(L)[None, :]
    lag = jnp.arange(L)
    sel_f = ((t_idx - s_idx)[:, :, None] == lag).astype(F32)
    sel_b = ((s_idx - t_idx)[:, :, None] == lag).astype(F32)
    t_small = (jnp.einsum('stl,lqhgxc->qhsgctx', sel_f, kern[:, 0], precision=hp)
               + jnp.einsum('stl,lqhgxc->qhsgctx', sel_b, kern[:, 1], precision=hp))
    t_small = t_small.reshape(N_SG, 2, HALF_W, LANES)

    def cmul(p_re, p_im, w_re, w_im):
        return p_re * w_re - p_im * w_im, p_re * w_im + p_im * w_re

    dn_re, dn_im = apow(L - jnp.arange(L + 1, dtype=F32)[:, None, None, None])

    def powers(lo, d, descending):
        if descending:
            return dn_re[L + 1 - lo - L:L + 1 - lo, d], dn_im[L + 1 - lo - L:L + 1 - lo, d]
        return pw_re[lo:lo + L, d], pw_im[lo:lo + L, d]

    parts = []
    for d, flip in ((0, True), (1, False)):
        p_re, p_im = powers(0, d, flip)
        parts += list(cmul(p_re[:, :, None, :], p_im[:, :, None, :],
                           bb_re[d].transpose(0, 2, 1)[None], bb_im[d].transpose(0, 2, 1)[None]))
    wb_small = jnp.stack(parts, axis=3)
    wb_small = wb_small.reshape(L, N_SG, 2, HALF_GROUPS, SSM_GROUP, 4 * STATE_P).transpose(1, 2, 0, 3, 4, 5)
    wb_small = wb_small.reshape(N_SG, 2, HALF_W, 4 * STATE_P)

    parts = []
    for d, flip in ((0, False), (1, True)):
        p_re, p_im = powers(1, d, flip)
        z_re, z_im = cmul(p_re[:, :, None, :], p_im[:, :, None, :], cm_re[d][None], cm_im[d][None])
        parts += [z_re, -z_im]
    wc_small = jnp.stack(parts, axis=0).reshape(4, L, N_SG, 2, HALF_GROUPS, SSM_GROUP, STATE_P)
    wc_small = wc_small.transpose(2, 3, 0, 4, 6, 1, 5).reshape(N_SG, 2, 4 * HALF_STATE, LANES)

    lane = jnp.arange(LANES)
    col = jnp.arange(HALF_W)
    rep_lane = ((lane[:, None] // SSM_GROUP == col[None, :] // HALF_LANES)
                & (lane[:, None] % SSM_GROUP == col[None, :] % SSM_GROUP))
    st = jnp.arange(4 * STATE_P)
    scol = jnp.arange(4 * HALF_STATE)
    rep_state = ((st[:, None] // STATE_P == scol[None, :] // HALF_STATE)
                 & (st[:, None] % STATE_P == scol[None, :] % STATE_P))
    lane_group = (col % HALF_LANES) // SSM_GROUP
    state_group = (scol % HALF_STATE) // STATE_P

    def expand(small, rep, row_group, col_group):
        big = jnp.einsum('qhrk,kc->qhrc', small.astype(BF16), rep.astype(BF16), preferred_element_type=F32)
        return jnp.where(row_group[:, None] == col_group[None, :], big, 0.0).astype(BF16)

    t_mat = expand(t_small, rep_lane, lane_group, lane_group)
    wb_mat = expand(wb_small, rep_state, lane_group, state_group)
    wc_mat = expand(wc_small, rep_lane, state_group, lane_group)

    row = jnp.arange(SCAN_ROWS)
    sh = jnp.array([1, 2, 4])
    ones = jnp.ones((1, SCAN_ROWS), bool)
    keep = jnp.stack([jnp.concatenate([row[None, :] >= sh[:, None], ones]),
                      jnp.concatenate([row[None, :] <= SCAN_ROWS - 1 - sh[:, None], ones])])
    shifts = jnp.broadcast_to(sh[:, None], (3, SCAN_ROWS))
    expo = L * jnp.stack([jnp.concatenate([shifts, row[None, :] + 1]),
                          jnp.concatenate([shifts, SCAN_ROWS - row[None, :]])]).astype(F32)
    e = expo[:, :, :, None, None]
    mag = jnp.exp(ldt_re[:, None, None] * e)
    k5 = keep[:, :, :, None, None]
    cst = jnp.stack([jnp.where(k5, mag * jnp.cos(ldt_im[:, None, None] * e), 0.0),
                     jnp.where(k5, mag * jnp.sin(ldt_im[:, None, None] * e), 0.0)], axis=2)
    cst = cst.reshape(2, 4, 2, SCAN_ROWS, N_SG, SG_STATE).transpose(4, 0, 1, 2, 3, 5)

    dsk = jnp.tile(d_skip.astype(F32).reshape(N_SG, 2, 1, HALF_LANES), (1, 1, L, 1)).reshape(N_SG, 1, L * LANES)
    return t_mat, wb_mat, wc_mat, cst, dsk


def _ssm_kernel(u_ref, t_ref, wb_ref, wc_ref, cst_ref, d_ref, y_ref, stf_ref, stb_ref, *, n_chunks, mm_rows):
    n_mm = n_chunks // mm_rows
    tiles = mm_rows // SCAN_ROWS
    half = SG_STATE
    row_id = lax.broadcasted_iota(I32, (SCAN_ROWS, half), 0)
    zero = jnp.zeros((1, half), F32)

    def block(c):
        return pl.ds(pl.multiple_of(c * mm_rows, mm_rows), mm_rows)

    def scan_tile(st_ref, d, row0, carry):
        rows = pl.ds(pl.multiple_of(row0, SCAN_ROWS), SCAN_ROWS)
        hs = HALF_STATE
        xr = jnp.concatenate([st_ref[rows, 0:hs], st_ref[rows, 2 * hs:3 * hs]], axis=1)
        xi = jnp.concatenate([st_ref[rows, hs:2 * hs], st_ref[rows, 3 * hs:4 * hs]], axis=1)
        for si, sh in enumerate((1, 2, 4)):
            ar, ai = cst_ref[d, si, 0], cst_ref[d, si, 1]
            shift = sh if d == 0 else SCAN_ROWS - sh
            pr, pi = pltpu.roll(xr, shift, 0), pltpu.roll(xi, shift, 0)
            xr, xi = xr + (ar * pr - ai * pi), xi + (ar * pi + ai * pr)
        cr, ci = carry
        ar, ai = cst_ref[d, 3, 0], cst_ref[d, 3, 1]
        xr, xi = xr + (ar * cr - ai * ci), xi + (ar * ci + ai * cr)
        if d == 0:
            edge, shift, last = 0, 1, SCAN_ROWS - 1
        else:
            edge, shift, last = SCAN_ROWS - 1, SCAN_ROWS - 1, 0
        sr = jnp.where(row_id == edge, cr, pltpu.roll(xr, shift, 0))
        si_ = jnp.where(row_id == edge, ci, pltpu.roll(xi, shift, 0))
        st_ref[rows, 0:hs], st_ref[rows, 2 * hs:3 * hs] = sr[:, :hs], sr[:, hs:]
        st_ref[rows, hs:2 * hs], st_ref[rows, 3 * hs:4 * hs] = si_[:, :hs], si_[:, hs:]
        return xr[last:last + 1, :], xi[last:last + 1, :]

    halves = [(h, slice(h * HALF_W, (h + 1) * HALF_W)) for h in range(2)]

    def inject(st_ref, d, c, u=None):
        for h, cols in halves:
            uh = u_ref[block(c), cols] if u is None else u[:, cols]
            st_ref[block(c), cols] = _dot(uh, wb_ref[h, :, 2 * d * HALF_STATE:2 * (d + 1) * HALF_STATE])

    def eject(st_ref, d, c):
        for h, cols in halves:
            y_ref[block(c), cols] += _dot(st_ref[block(c), cols].astype(BF16),
                                          wc_ref[h, 2 * d * HALF_STATE:2 * (d + 1) * HALF_STATE, :])

    def inject_fwd(c, carry):
        inject(stf_ref, 0, c)
        return carry

    lax.fori_loop(0, n_mm, inject_fwd, 0)

    def forward(c, carry):
        u = u_ref[block(c), :]
        for h, cols in halves:
            y_ref[block(c), cols] = _dot(u[:, cols], t_ref[h]) + u[:, cols].astype(F32) * d_ref[:, cols]
        inject(stb_ref, 1, c, u)
        for k in range(tiles):
            carry = scan_tile(stf_ref, 0, c * mm_rows + k * SCAN_ROWS, carry)
        return carry

    lax.fori_loop(0, n_mm, forward, (zero, zero))

    def backward(i, carry):
        c = n_mm - 1 - i
        eject(stf_ref, 0, c)
        for k in reversed(range(tiles)):
            carry = scan_tile(stb_ref, 1, c * mm_rows + k * SCAN_ROWS, carry)
        return carry

    lax.fori_loop(0, n_mm, backward, (zero, zero))

    def eject_bwd(c, carry):
        eject(stb_ref, 1, c)
        return carry

    lax.fori_loop(0, n_mm, eject_bwd, 0)


def _ssm(u, tabs):
    t_mat, wb_mat, wc_mat, cst, dsk = tabs
    b, _, n_chunks, lw = u.shape
    mm_rows = min(256, n_chunks)
    assert n_chunks % mm_rows == 0 and n_chunks % SCAN_ROWS == 0
    one = pl.Buffered(1)
    wspec = lambda shape: pl.BlockSpec((None,) + shape, lambda q, i: (q,) + tuple(0 for _ in shape),
                                       pipeline_mode=one)
    return pl.pallas_call(
        functools.partial(_ssm_kernel, n_chunks=n_chunks, mm_rows=mm_rows),
        grid=(N_SG, b),
        in_specs=[pl.BlockSpec((None, None, n_chunks, lw), lambda q, i: (i, q, 0, 0)),
                  wspec((2, HALF_W, HALF_W)), wspec((2, HALF_W, 4 * HALF_STATE)), wspec((2, 4 * HALF_STATE, HALF_W)),
                  wspec((2, 4, 2, SCAN_ROWS, SG_STATE)), wspec((1, lw))],
        out_specs=pl.BlockSpec((None, None, n_chunks, lw), lambda q, i: (i, q, 0, 0)),
        out_shape=jax.ShapeDtypeStruct((b, N_SG, n_chunks, lw), F32),
        scratch_shapes=[pltpu.VMEM((n_chunks, 2 * SG_STATE), F32)] * 2,
        compiler_params=_cparams(("arbitrary", "arbitrary")),
        name="ssm",
    )(u, t_mat, wb_mat, wc_mat, cst, dsk)


def _merge_kernel(x_ref, attn_ref, y_ref, ga_ref, gs_ref, g1_ref, sc_ref, sh_ref, nf_ref,
                  wab_ref, wglu_ref, wout_ref, wr_ref, x1_ref, h2_ref, lg_ref, gel_scr):
    sub = x_ref.shape[0] // MERGE_SPLIT
    crows = sub // SSM_L
    blocks = [(k, slice(k * sub, (k + 1) * sub)) for k in range(MERGE_SPLIT)]

    ab = [_dot(attn_ref[rows, :], wab_ref[...]) for _, rows in blocks]

    low = lax.broadcasted_iota(I32, (crows, LANES), 1) < HALF_LANES
    gel = []
    for k, rows in blocks:
        for sg in range(N_SG):
            for sp in range(SSM_L // 2):
                t0 = y_ref[sg, k * crows:(k + 1) * crows, sp * LANES:(sp + 1) * LANES]
                t1 = y_ref[sg, k * crows:(k + 1) * crows, HALF_W + sp * LANES:HALF_W + (sp + 1) * LANES]
                pair = (jnp.where(low, t0, pltpu.roll(t1, HALF_LANES, 1)),
                        jnp.where(low, pltpu.roll(t0, HALF_LANES, 1), t1))
                for s, y in zip((2 * sp, 2 * sp + 1), pair):
                    gel_scr[sg, k * sub + s * crows:k * sub + (s + 1) * crows, :] = (
                        0.5 * y * (1.0 + jnp.tanh(math.sqrt(2.0 / math.pi) * (y + 0.044715 * (y * y * y)))))
        gel.append(jnp.concatenate(
            [jnp.concatenate([gel_scr[sg, pl.ds(k * sub + j, SSM_L, stride=crows), :] for j in range(crows)], axis=0)
             for sg in range(N_SG)], axis=1).astype(BF16))
    glu = [_dot(g, wglu_ref[...]) for g in gel]
    merged = []
    for (k, rows), ab_k, glu_k in zip(blocks, ab, glu):
        sb = glu_k[:, :D_MODEL] * _sigmoid(glu_k[:, D_MODEL:])
        merged.append((ga_ref[rows, :].astype(F32) * ab_k + gs_ref[rows, :].astype(F32) * sb).astype(BF16))
    mixed = [_dot(m, wout_ref[...]) for m in merged]
    h2s = []
    for (k, rows), mix in zip(blocks, mixed):
        x1 = x_ref[rows, :] + g1_ref[...] * mix
        x1_ref[rows, :] = x1
        ms = jnp.mean(x1 * x1, axis=-1, keepdims=True)
        h2 = x1 * lax.rsqrt(ms + EPS) * nf_ref[...] * (1.0 + sc_ref[...]) + sh_ref[...]
        h2_ref[rows, :D_MODEL] = h2.astype(BF16)
        h2s.append(h2)
    for (k, rows), h2 in zip(blocks, h2s):
        lg = _dot3(h2, wr_ref[...])
        lg_ref[rows, :] = lg
        valid = lax.broadcasted_iota(I32, lg.shape, 1) < N_EXPERTS
        m = jnp.max(jnp.where(valid, lg, -jnp.inf), axis=-1, keepdims=True)
        ex = jnp.where(valid, jnp.exp(lg - m), 0.0)
        aff = ex / jnp.sum(ex, axis=-1, keepdims=True)
        hi = aff.astype(BF16).astype(F32)
        mid = (aff - hi).astype(BF16).astype(F32)
        lo = (aff - hi - mid).astype(BF16).astype(F32)
        parts = hi + pltpu.roll(mid, N_EXPERTS, 1) + pltpu.roll(lo, 2 * N_EXPERTS, 1)
        h2_ref[rows, D_MODEL:] = parts.astype(BF16)


def _merge(x, attn, y4, ga, gs, g1, sc2, sh2, norm_ffn, wab, wglu, wout, wr_pad, bt):
    b, n, d = x.shape
    tok = lambda w: pl.BlockSpec((None, bt, w), lambda i, j: (i, j, 0))
    mod = pl.BlockSpec((None, 1, d), lambda i, j: (i, 0, 0))
    full = lambda shape: pl.BlockSpec(shape, lambda i, j: tuple(0 for _ in shape))
    return pl.pallas_call(
        _merge_kernel,
        grid=(b, n // bt),
        in_specs=[tok(d), tok(D_ATTN),
                  pl.BlockSpec((None, N_SG, bt // SSM_L, SSM_L * LANES), lambda i, j: (i, 0, j, 0)),
                  tok(d), tok(d), mod, mod, mod, full((1, d)),
                  full((D_ATTN, d)), full((D_SSM, 2 * d)), full((d, d)), full((d, LANES))],
        out_specs=[tok(d), tok(d + LANES), tok(LANES)],
        out_shape=[jax.ShapeDtypeStruct((b, n, d), F32), jax.ShapeDtypeStruct((b, n, d + LANES), BF16),
                   jax.ShapeDtypeStruct((b, n, LANES), F32)],
        scratch_shapes=[pltpu.VMEM((N_SG, bt, LANES), F32)],
        compiler_params=_cparams(("parallel", "parallel")),
        name="merge",
    )(x, attn, y4, ga, gs, g1, sc2, sh2, norm_ffn, wab, wglu, wout, wr_pad)


def _route_kernel(lg_ref, tri_ref, slot_ref, cnt_ref, aff_ref, *, cap):
    lg = lg_ref[...]
    m = jnp.max(lg, axis=1, keepdims=True)
    e = jnp.exp(lg - m)
    aff = e / jnp.sum(e, axis=1, keepdims=True)
    aff_ref[...] = aff

    def count(mask):
        c = jnp.sum(jnp.where(mask, 1.0, 0.0), axis=0, keepdims=True)
        return jnp.sum(c, axis=2, keepdims=True)

    def as_float(bits):
        return lax.bitcast_convert_type(bits, F32)

    def bit_step(i, thr):
        cand = thr | (jnp.int32(1) << (30 - i))
        return jnp.where(count(aff >= as_float(cand)) >= cap, cand, thr)

    thr3 = as_float(lax.fori_loop(0, 31, bit_step, jnp.zeros((1, N_EXPERTS, 1), I32)))
    need = (cap - count(aff > thr3))[0]
    thr = thr3[0]
    n_chunks = lg.shape[0]
    carry0 = jnp.zeros((N_EXPERTS, 1), F32)

    def prefix(flag, carry):
        f = jnp.where(flag, 1.0, 0.0)
        inc = _dot(f.astype(BF16), tri_ref[...]) + carry
        return inc, inc - f

    def tie_body(c, carry):
        a = aff_ref[c]
        tie = a == thr
        inc, rank = prefix(tie, carry)
        sel = (a > thr) | (tie & (rank < need))
        slot_ref[c] = jnp.where(sel, 1, 0).astype(I32)
        return inc[:, CUM_W - 1:CUM_W]

    lax.fori_loop(0, n_chunks, tie_body, carry0)

    def slot_body(c, carry):
        sel = slot_ref[c] > 0
        inc, excl = prefix(sel, carry)
        slot_ref[c] = jnp.where(sel, excl, -1.0).astype(I32)
        cnt_ref[c] = excl.astype(I32)
        return inc[:, CUM_W - 1:CUM_W]

    lax.fori_loop(0, n_chunks, slot_body, carry0)


def _route(logits_c, tri, cap):
    nc = logits_c.shape[0]
    shp = (nc, N_EXPERTS, CUM_W)
    return pl.pallas_call(
        functools.partial(_route_kernel, cap=cap),
        grid=(1,),
        in_specs=[pl.BlockSpec(shp, lambda i: (0, 0, 0)), pl.BlockSpec((CUM_W, CUM_W), lambda i: (0, 0))],
        out_specs=[pl.BlockSpec(shp, lambda i: (0, 0, 0))] * 2,
        out_shape=[jax.ShapeDtypeStruct(shp, I32)] * 2,
        scratch_shapes=[pltpu.VMEM(shp, F32)],
        compiler_params=_cparams(("arbitrary",)),
        name="route",
    )(logits_c, tri)


def _gather_kernel(offs_ref, h_ref, slot_ref, xe_ref, *, nb, cap, win, bt):
    eg = pl.program_id(0)
    step = pl.program_id(1)

    @pl.when(step == 0)
    def _():
        xe_ref[...] = jnp.zeros_like(xe_ref)

    riota = lax.broadcasted_iota(I32, (win, 1), 0)
    blocks_per_step = h_ref.shape[0] // bt

    for sb in range(blocks_per_step):
        blk = step * blocks_per_step + sb
        rows = slice(sb * bt, (sb + 1) * bt)

        def window(ee, w, blk=blk, rows=rows):
            off = offs_ref[(eg * GATHER_EXPERTS + ee) * (nb + 1) + blk]
            nominal = (off // BF16_ROWS) * BF16_ROWS + w * win
            start = pl.multiple_of(jnp.minimum(nominal, cap - win), BF16_ROWS)
            slot = slot_ref[ee, :, rows]
            hit = (slot == riota + start) & (slot >= nominal)
            return jnp.where(hit, 1.0, 0.0).astype(BF16), start

        firsts = [window(ee, 0) for ee in range(GATHER_EXPERTS)]
        picked = _dot(jnp.concatenate([oh for oh, _ in firsts], axis=0), h_ref[rows, :]).astype(BF16)
        for ee, (_, start) in enumerate(firsts):
            xe_ref[ee, pl.ds(start, win), :] += picked[ee * win:(ee + 1) * win]

        for ee in range(GATHER_EXPERTS):
            off = offs_ref[(eg * GATHER_EXPERTS + ee) * (nb + 1) + blk]
            end = offs_ref[(eg * GATHER_EXPERTS + ee) * (nb + 1) + blk + 1]
            n_win = (end - (off // BF16_ROWS) * BF16_ROWS + win - 1) // win

            def extra(w, carry, ee=ee, window=window, rows=rows):
                onehot, start = window(ee, w)
                xe_ref[ee, pl.ds(start, win), :] += _dot(onehot, h_ref[rows, :]).astype(BF16)
                return carry

            lax.fori_loop(1, n_win, extra, 0)


def _gather(offs_flat, h2, slot3, cap, bt):
    t, d = h2.shape
    nb = t // bt
    win = min(GATHER_WIN, cap)
    ge = GATHER_EXPERTS
    step_rows = bt * (GATHER_BLOCKS if nb % GATHER_BLOCKS == 0 else 1)
    gs = pltpu.PrefetchScalarGridSpec(
        num_scalar_prefetch=1,
        grid=(N_EXPERTS // ge, t // step_rows),
        in_specs=[pl.BlockSpec((step_rows, d), lambda e, j, offs: (j, 0)),
                  pl.BlockSpec((ge, 1, step_rows), lambda e, j, offs: (e, 0, j))],
        out_specs=pl.BlockSpec((ge, cap, d), lambda e, j, offs: (e, 0, 0), pipeline_mode=pl.Buffered(1)),
    )
    return pl.pallas_call(
        functools.partial(_gather_kernel, nb=nb, cap=cap, win=win, bt=bt),
        grid_spec=gs,
        out_shape=jax.ShapeDtypeStruct((N_EXPERTS, cap, d), BF16),
        compiler_params=_cparams(("arbitrary", "arbitrary")),
        name="gather",
    )(offs_flat, h2, slot3)


def _ffn_kernel(x_ref, wg_ref, wu_ref, wd_ref, y_ref, *, fchunk):
    x = x_ref[:, :D_MODEL]
    acc = jnp.zeros((x.shape[0], D_MODEL), F32)
    for f in range(D_EXPERT // fchunk):
        fs = slice(f * fchunk, (f + 1) * fchunk)
        a = _dot(x, wg_ref[:, fs])
        u = _dot(x, wu_ref[:, fs])
        hmid = (a * _sigmoid(a) * u).astype(BF16)
        acc = acc + _dot(hmid, wd_ref[fs, :])
    parts = x_ref[:, D_MODEL:].astype(F32)
    lane = lax.broadcasted_iota(I32, parts.shape, 1)
    mine = (lane % N_EXPERTS == pl.program_id(0)) & (lane < 3 * N_EXPERTS)
    gate = jnp.sum(jnp.where(mine, parts, 0.0), axis=-1, keepdims=True)
    y_ref[...] = (gate * acc).astype(BF16)


def _ffn(xe, wg, wu, wd, tm):
    e, cap, dx = xe.shape
    d = D_MODEL
    return pl.pallas_call(
        functools.partial(_ffn_kernel, fchunk=512),
        grid=(e, cap // tm),
        in_specs=[pl.BlockSpec((None, tm, dx), lambda i, j: (i, j, 0)),
                  pl.BlockSpec((None, d, D_EXPERT), lambda i, j: (i, 0, 0)),
                  pl.BlockSpec((None, d, D_EXPERT), lambda i, j: (i, 0, 0)),
                  pl.BlockSpec((None, D_EXPERT, d), lambda i, j: (i, 0, 0))],
        out_specs=pl.BlockSpec((None, tm, d), lambda i, j: (i, j, 0)),
        out_shape=jax.ShapeDtypeStruct((e, cap, d), BF16),
        compiler_params=_cparams(("parallel", "parallel")),
        name="ffn",
    )(xe, wg, wu, wd)


def _window_copy(ye_hbm, buf, sem, e, start, win):
    return pltpu.make_async_copy(ye_hbm.at[e, pl.ds(start, win), :], buf, sem)


def _combine_kernel(offs_ref, x1_ref, slot_ref, g2_ref, ye_hbm, o_ref, ybuf, xbuf, lhs, sems, xsem,
                    *, nb, cap, win):
    blk = pl.program_id(0)

    def first_window(b_, e):
        off = offs_ref[e * (nb + 1) + b_]
        a0 = (off // BF16_ROWS) * BF16_ROWS
        return a0, pl.multiple_of(jnp.minimum(a0, cap - win), BF16_ROWS)

    def window_copies(b_, half):
        return [_window_copy(ye_hbm, ybuf.at[half, pl.ds(e * win, win)], sems.at[half, e], e,
                             first_window(b_, e)[1], win) for e in range(N_EXPERTS)]

    @pl.when(blk == 0)
    def _():
        for cp in window_copies(0, 0):
            cp.start()

    @pl.when(blk + 1 < nb)
    def _():
        for cp in window_copies(blk + 1, (blk + 1) % 2):
            cp.start()

    starts = [first_window(blk, e) for e in range(N_EXPERTS)]
    liota = lax.broadcasted_iota(I32, (1, win), 1)
    slots = slot_ref[...]
    for e in range(N_EXPERTS):
        hit = slots[:, e:e + 1] == liota + starts[e][1]
        lhs[:, e * win:(e + 1) * win] = jnp.where(hit, 1.0, 0.0).astype(BF16)
    for cp in window_copies(blk, blk % 2):
        cp.wait()
    o_ref[...] = x1_ref[...] + g2_ref[...] * _dot(lhs[...], ybuf[blk % 2])

    for e in range(N_EXPERTS):
        a0 = starts[e][0]
        end = offs_ref[e * (nb + 1) + blk + 1]
        n_win = (end - a0 + win - 1) // win

        def extra(w, carry, e=e, a0=a0):
            nominal = a0 + w * win
            st = pl.multiple_of(jnp.minimum(nominal, cap - win), BF16_ROWS)
            cp = _window_copy(ye_hbm, xbuf, xsem, e, st, win)
            cp.start()
            cp.wait()
            scol = slot_ref[:, e:e + 1]
            hit = (scol == liota + st) & (scol >= nominal)
            o_ref[...] += g2_ref[...] * _dot(jnp.where(hit, 1.0, 0.0).astype(BF16), xbuf[...])
            return carry

        lax.fori_loop(1, n_win, extra, 0)


def _combine(offs_flat, x1, slot_t, g2, ye, n_per_batch, bt):
    t, d = x1.shape
    nb = t // bt
    cap = ye.shape[1]
    win = min(SLOT_WIN, cap)
    per = n_per_batch // bt
    gs = pltpu.PrefetchScalarGridSpec(
        num_scalar_prefetch=1,
        grid=(nb,),
        in_specs=[pl.BlockSpec((bt, d), lambda j, offs: (j, 0)),
                  pl.BlockSpec((bt, N_EXPERTS), lambda j, offs: (j, 0)),
                  pl.BlockSpec((None, 1, d), lambda j, offs: (j // per, 0, 0)),
                  pl.BlockSpec(memory_space=pl.ANY)],
        out_specs=pl.BlockSpec((bt, d), lambda j, offs: (j, 0)),
        scratch_shapes=[pltpu.VMEM((2, N_EXPERTS * win, d), BF16), pltpu.VMEM((win, d), BF16),
                        pltpu.VMEM((bt, N_EXPERTS * win), BF16),
                        pltpu.SemaphoreType.DMA((2, N_EXPERTS)), pltpu.SemaphoreType.DMA(())],
    )
    return pl.pallas_call(
        functools.partial(_combine_kernel, nb=nb, cap=cap, win=win),
        grid_spec=gs,
        out_shape=jax.ShapeDtypeStruct((t, d), F32),
        compiler_params=_cparams(("arbitrary",)),
        name="combine",
    )(offs_flat, x1, slot_t, g2, ye)


def _prep_weights(w_in, q_norm, k_norm, rpb, ssm_params, w_glu, w_attn_br, w_out, w_router,
                  w_exp_gate, w_exp_up, w_exp_down):
    head = jnp.arange(D_ATTN) // HEAD_DIM
    return dict(
        w_in=w_in.astype(BF16),
        qg=jnp.tile(q_norm.astype(F32), N_HEADS).reshape(1, D_ATTN),
        kg=jnp.tile(k_norm.astype(F32), N_HEADS).reshape(1, D_ATTN),
        ones_bd=(head[:, None] == head[None, :]).astype(BF16),
        bias_tab=_attn_bias_table(rpb),
        ssm=_ssm_tables(*ssm_params),
        wglu=w_glu.astype(BF16), wab=w_attn_br.astype(BF16), wout=w_out.astype(BF16),
        wr=jnp.pad(w_router.astype(F32), ((0, 0), (0, LANES - N_EXPERTS))),
        wg=w_exp_gate.astype(BF16), wu=w_exp_up.astype(BF16), wd=w_exp_down.astype(BF16),
        tri=(jnp.arange(CUM_W)[:, None] <= jnp.arange(CUM_W)[None, :]).astype(BF16),
    )


def _token_block(n, want):
    bt = min(want, n)
    assert n % bt == 0
    return bt


def _encoder_layer(x, c, w_ada, b_ada, norm_mix, norm_ffn, wts):
    b, n, d = x.shape
    t = b * n
    cap = EC_CAPACITY * t // N_EXPERTS

    c_pad = jnp.pad(c.astype(F32), ((0, (-b) % SUBLANES), (0, 0)))
    mod = _ada(c_pad, w_ada, b_ada)[:b]
    sh1, sc1, g1, sh2, sc2, g2 = [m.reshape(b, 1, d) for m in jnp.split(mod, 6, axis=-1)]

    bt = _token_block(n, 512)
    q, k, v, u4, ga, gs = _inproj(x, sc1, sh1, norm_mix.reshape(1, d), wts["w_in"], wts["qg"], wts["kg"],
                                  wts["ones_bd"], bt)
    attn = _attention(q, k, v, wts["bias_tab"])
    y4 = _ssm(u4, wts["ssm"])
    x1, h2, logits = _merge(x, attn, y4, ga, gs, g1, sc2, sh2, norm_ffn.reshape(1, d),
                            wts["wab"], wts["wglu"], wts["wout"], wts["wr"], bt)

    lg = logits.reshape(t, LANES)[:, :N_EXPERTS]
    lg_c = lg.reshape(t // CUM_W, CUM_W, N_EXPERTS).transpose(0, 2, 1)
    slot_c, cnt_c = _route(lg_c, wts["tri"], cap)
    slot_et = slot_c.transpose(1, 0, 2).reshape(N_EXPERTS, t)
    slot_te = slot_et.T

    bt2 = _token_block(t, 512)
    cnt_at_block = cnt_c.transpose(1, 0, 2).reshape(N_EXPERTS, t)[:, ::bt2]
    offs = jnp.concatenate([cnt_at_block, jnp.full((N_EXPERTS, 1), cap, I32)], axis=-1)
    offs_flat = offs.reshape(-1).astype(I32)

    xe = _gather(offs_flat, h2.reshape(t, d + LANES), slot_et.reshape(N_EXPERTS, 1, t), cap, bt2)
    ye = _ffn(xe, wts["wg"], wts["wu"], wts["wd"], _token_block(cap, 1024))
    out = _combine(offs_flat, x1.reshape(t, d), slot_te, g2, ye, n, bt2)
    return out.reshape(b, n, d)


def kernel(x_prompt, x_sample, c_prompt, c_sample, w_ada, b_ada, norm_mix, norm_ffn, w_in, q_norm, k_norm, rpb,
           ssm_a_re, ssm_a_im, ssm_log_dt, ssm_b_re, ssm_b_im, ssm_c_re, ssm_c_im, ssm_d, w_glu, w_attn_br,
           w_out, w_router, w_exp_gate, w_exp_up, w_exp_down):
    y_prompt, y_sample = x_prompt, x_sample
    for layer in range(w_ada.shape[0]):
        ssm_params = tuple(p[layer] for p in (ssm_a_re, ssm_a_im, ssm_log_dt, ssm_b_re, ssm_b_im,
                                              ssm_c_re, ssm_c_im, ssm_d))
        wts = _prep_weights(w_in[layer], q_norm[layer], k_norm[layer], rpb[layer], ssm_params, w_glu[layer],
                            w_attn_br[layer], w_out[layer], w_router[layer], w_exp_gate[layer],
                            w_exp_up[layer], w_exp_down[layer])
        y_prompt = _encoder_layer(y_prompt, c_prompt, w_ada[layer], b_ada[layer], norm_mix[layer],
                                  norm_ffn[layer], wts)
        y_sample = _encoder_layer(y_sample, c_sample, w_ada[layer], b_ada[layer], norm_mix[layer],
                                  norm_ffn[layer], wts)
    return (y_prompt, y_sample)
```

```python
import functools
import math

import jax
import jax.numpy as jnp
from jax import lax
from jax.experimental import pallas as pl
from jax.experimental.pallas import tpu as pltpu

F32 = jnp.float32
BF16 = jnp.bfloat16
I32 = jnp.int32

D_MODEL = 1024
GRID_W = 64
N_HEADS = 8
HEAD_DIM = 64
D_ATTN = N_HEADS * HEAD_DIM
WIN_R = 8
WIN_C = 16
SSM_GROUP = 16
D_SSM = 512
N_GROUPS = D_SSM // SSM_GROUP
STATE_P = 64
D_IN = 3 * D_ATTN + D_SSM + 2 * D_MODEL
N_EXPERTS = 16
EC_CAPACITY = 2
D_EXPERT = 2048
EPS = 1e-6
NEG_INF = -1e9

LANES = 128
SUBLANES = 8
BF16_ROWS = 16
VMEM_LIMIT = 56 * 1024 * 1024

SSM_L = 8
SG_GROUPS = LANES // SSM_GROUP
N_SG = N_GROUPS // SG_GROUPS
SG_STATE = SG_GROUPS * STATE_P
HALF_GROUPS = SG_GROUPS // 2
HALF_LANES = LANES // 2
HALF_W = SSM_L * HALF_LANES
HALF_STATE = HALF_GROUPS * STATE_P
SCAN_ROWS = SUBLANES

ATTN_ROWS = 8
ATTN_UNROLL = 4
MERGE_SPLIT = 4
INPROJ_SPLIT = 1
SLOT_WIN = 128
GATHER_WIN = 96
GATHER_BLOCKS = 4
GATHER_EXPERTS = 4
CUM_W = 256


def _cparams(sem):
    return pltpu.CompilerParams(dimension_semantics=sem, vmem_limit_bytes=VMEM_LIMIT)


def _split_bf16(a):
    hi = a.astype(BF16)
    lo = (a - hi.astype(F32)).astype(BF16)
    return hi, lo


def _dot(a, b):
    return jnp.dot(a, b, preferred_element_type=F32)


def _dot3(a, b):
    ah, al = _split_bf16(a)
    bh, bl = _split_bf16(b)
    return _dot(ah, bh) + (_dot(ah, bl) + _dot(al, bh))


def _sigmoid(z):
    return 1.0 / (1.0 + jnp.exp(-z))


def _ada_kernel(c_ref, w_ref, b_ref, o_ref):
    c = c_ref[...]
    s = c * _sigmoid(c)
    o_ref[...] = _dot3(s, w_ref[...]) + b_ref[...]


def _ada(c_pad, w_ada, b_ada):
    rows = c_pad.shape[0]
    n_out = w_ada.shape[1]
    return pl.pallas_call(
        _ada_kernel,
        grid=(n_out // D_MODEL,),
        in_specs=[pl.BlockSpec((rows, D_MODEL), lambda j: (0, 0)),
                  pl.BlockSpec((D_MODEL, D_MODEL), lambda j: (0, j)),
                  pl.BlockSpec((1, D_MODEL), lambda j: (0, j))],
        out_specs=pl.BlockSpec((rows, D_MODEL), lambda j: (0, j)),
        out_shape=jax.ShapeDtypeStruct((rows, n_out), F32),
        compiler_params=_cparams(("arbitrary",)),
        name="ada",
    )(c_pad, w_ada, b_ada.reshape(1, n_out))


def _inproj_kernel(x_ref, sc_ref, sh_ref, nm_ref, w_ref, qg_ref, kg_ref, ones_ref,
                   q_ref, k_ref, v_ref, u_ref, ga_ref, gs_ref, h_scr):
    def head_norm(z, gain):
        ssum = _dot((z * z).astype(BF16), ones_ref[...])
        return z * lax.rsqrt(ssum * (1.0 / HEAD_DIM) + EPS) * gain

    sub = x_ref.shape[0] // INPROJ_SPLIT
    crows = sub // SSM_L
    for kb in range(INPROJ_SPLIT):
        rows = slice(kb * sub, (kb + 1) * sub)
        x = x_ref[rows, :]
        ms = jnp.mean(x * x, axis=-1, keepdims=True)
        xn = x * lax.rsqrt(ms + EPS) * nm_ref[...]
        hf = xn * (1.0 + sc_ref[...]) + sh_ref[...]
        for c in range(D_MODEL // LANES):
            h_scr[c, rows, :] = hf[:, c * LANES:(c + 1) * LANES]
        h = hf.astype(BF16)

        def proj(lo, hi, h=h):
            return _dot(h, w_ref[:, lo:hi])

        q = head_norm(proj(0, D_ATTN), qg_ref[...]) * (HEAD_DIM ** -0.5)
        q_ref[rows, :] = q.astype(BF16)
        k = head_norm(proj(D_ATTN, 2 * D_ATTN), kg_ref[...])
        k_ref[rows, :] = k.astype(BF16)
        v_ref[rows, :] = proj(2 * D_ATTN, 3 * D_ATTN).astype(BF16)
        hp = jnp.concatenate(
            [jnp.concatenate([h_scr[c, pl.ds(kb * sub + s, crows, stride=SSM_L), :]
                              for c in range(D_MODEL // LANES)], axis=1)
             for s in range(SSM_L)], axis=0)
        u = _dot(hp.astype(BF16), w_ref[:, 3 * D_ATTN:3 * D_ATTN + D_SSM])
        low = lax.broadcasted_iota(I32, (crows, LANES), 1) < HALF_LANES
        out_rows = slice(kb * crows, (kb + 1) * crows)
        for sg in range(N_SG):
            for sp in range(SSM_L // 2):
                a = u[(2 * sp) * crows:(2 * sp + 1) * crows, sg * LANES:(sg + 1) * LANES]
                b = u[(2 * sp + 1) * crows:(2 * sp + 2) * crows, sg * LANES:(sg + 1) * LANES]
                u_ref[sg, out_rows, sp * LANES:(sp + 1) * LANES] = (
                    jnp.where(low, a, pltpu.roll(b, HALF_LANES, 1)).astype(BF16))
                u_ref[sg, out_rows, HALF_W + sp * LANES:HALF_W + (sp + 1) * LANES] = (
                    jnp.where(low, pltpu.roll(a, HALF_LANES, 1), b).astype(BF16))
        o = 3 * D_ATTN + D_SSM
        ga_ref[rows, :] = _sigmoid(proj(o, o + D_MODEL)).astype(BF16)
        gs_ref[rows, :] = _sigmoid(proj(o + D_MODEL, o + 2 * D_MODEL)).astype(BF16)


def _inproj(x, sc1, sh1, norm_mix, w_in_bf, qg, kg, ones_bd, bt):
    b, n, d = x.shape
    tok = lambda w: pl.BlockSpec((None, bt, w), lambda i, j: (i, j, 0))
    mod = pl.BlockSpec((None, 1, d), lambda i, j: (i, 0, 0))
    full = lambda shape: pl.BlockSpec(shape, lambda i, j: tuple(0 for _ in shape))
    return pl.pallas_call(
        _inproj_kernel,
        grid=(b, n // bt),
        in_specs=[tok(d), mod, mod, full((1, d)), full((d, D_IN)),
                  full((1, D_ATTN)), full((1, D_ATTN)), full((D_ATTN, D_ATTN))],
        out_specs=[tok(D_ATTN), tok(D_ATTN), tok(D_ATTN),
                   pl.BlockSpec((None, N_SG, bt // SSM_L, SSM_L * LANES), lambda i, j: (i, 0, j, 0)),
                   tok(d), tok(d)],
        out_shape=[jax.ShapeDtypeStruct((b, n, D_ATTN), BF16)] * 3
        + [jax.ShapeDtypeStruct((b, N_SG, n // SSM_L, SSM_L * LANES), BF16)]
        + [jax.ShapeDtypeStruct((b, n, d), BF16)] * 2,
        scratch_shapes=[pltpu.VMEM((d // LANES, bt, LANES), F32)],
        compiler_params=_cparams(("parallel", "parallel")),
        name="inproj",
    )(x, sc1, sh1, norm_mix, w_in_bf, qg, kg, ones_bd)


def _attn_kernel(q_ref, kp_ref, kc_ref, kn_ref, vp_ref, vc_ref, vn_ref, bias_ref, o_ref,
                 kwin, vwin, *, rows):
    r0 = pl.program_id(1) * ATTN_ROWS
    rw = ATTN_ROWS * GRID_W
    for t, (kr, vr) in enumerate(((kp_ref, vp_ref), (kc_ref, vc_ref), (kn_ref, vn_ref))):
        kwin[t * rw:(t + 1) * rw, :] = kr[...]
        vwin[t * rw:(t + 1) * rw, :] = vr[...]
    even = lax.broadcasted_iota(I32, (GRID_W, LANES), 1) < HEAD_DIM
    nkeys = WIN_R * GRID_W
    pairs = N_HEADS // 2

    def rows_body(it, carry):
        units = []
        for sub in range(ATTN_UNROLL):
            i = it * ATTN_UNROLL + sub
            r = r0 + i
            rs = jnp.clip(r - WIN_R // 2, 0, rows - WIN_R)
            variant = r - rs
            koff = pl.multiple_of((rs - r0 + ATTN_ROWS) * GRID_W, GRID_W)
            qoff = pl.multiple_of(i * GRID_W, GRID_W)
            for hp in range(pairs):
                ls = slice(hp * LANES, (hp + 1) * LANES)
                qp = q_ref[pl.ds(qoff, GRID_W), ls]
                zero = jnp.zeros_like(qp)
                q2 = jnp.concatenate([jnp.where(even, qp, zero), jnp.where(even, zero, qp)], axis=0)
                kp = kwin[pl.ds(koff, nkeys), ls]
                s = lax.dot_general(q2, kp, (((1,), (1,)), ((), ())), preferred_element_type=F32)
                units.append((qoff, koff, ls, s + bias_ref[variant, hp]))
        probs = []
        for qoff, koff, ls, s in units:
            p = jnp.exp(s - jnp.max(s, axis=-1, keepdims=True))
            probs.append((p.astype(BF16), jnp.sum(p, axis=-1, keepdims=True)))
        for (qoff, koff, ls, _), (p, l) in zip(units, probs):
            o2 = _dot(p, vwin[pl.ds(koff, nkeys), ls]) / l
            o_ref[pl.ds(qoff, GRID_W), ls] = jnp.where(even, o2[:GRID_W], o2[GRID_W:]).astype(BF16)
        return carry

    lax.fori_loop(0, ATTN_ROWS // ATTN_UNROLL, rows_body, 0)


def _attention(q, k, v, bias_tab):
    b, n, _ = q.shape
    rows = n // GRID_W
    assert rows % ATTN_ROWS == 0 and rows >= WIN_R
    nblk = rows // ATTN_ROWS
    rw = ATTN_ROWS * GRID_W
    cur = pl.BlockSpec((None, rw, D_ATTN), lambda i, j: (i, j, 0))
    prv = pl.BlockSpec((None, rw, D_ATTN), lambda i, j: (i, jnp.maximum(j - 1, 0), 0))
    nxt = pl.BlockSpec((None, rw, D_ATTN), lambda i, j: (i, jnp.minimum(j + 1, nblk - 1), 0))
    return pl.pallas_call(
        functools.partial(_attn_kernel, rows=rows),
        grid=(b, nblk),
        in_specs=[cur, prv, cur, nxt, prv, cur, nxt,
                  pl.BlockSpec(bias_tab.shape, lambda i, j: (0, 0, 0, 0))],
        out_specs=cur,
        out_shape=jax.ShapeDtypeStruct((b, n, D_ATTN), BF16),
        scratch_shapes=[pltpu.VMEM((3 * rw, D_ATTN), BF16), pltpu.VMEM((3 * rw, D_ATTN), BF16)],
        compiler_params=_cparams(("parallel", "parallel")),
        name="attn",
    )(q, k, k, k, v, v, v, bias_tab)


def _attn_bias_table(rpb):
    var = jnp.arange(WIN_R)
    a = jnp.arange(WIN_R)
    j = jnp.arange(GRID_W)
    c_start = jnp.clip(j - WIN_C // 2, 0, GRID_W - WIN_C)
    col_ok = (j[None, :] >= c_start[:, None]) & (j[None, :] < c_start[:, None] + WIN_C)
    col_off = jnp.clip(j[None, :] - j[:, None], -(WIN_C - 1), WIN_C - 1) + WIN_C - 1
    row_off = a[None, :] - var[:, None] + WIN_R - 1
    hp = lax.Precision.HIGHEST
    row_sel = (row_off[:, :, None] == jnp.arange(2 * WIN_R - 1)).astype(F32)
    col_sel = (col_off[:, :, None] == jnp.arange(2 * WIN_C - 1)).astype(F32)
    tab = jnp.einsum('hrc,var->hvac', rpb.astype(F32), row_sel, precision=hp)
    tab = jnp.einsum('hvac,jkc->vhjak', tab, col_sel, precision=hp)
    tab = jnp.where(col_ok[None, None, :, None, :], tab, NEG_INF)
    return tab.reshape(WIN_R, N_HEADS // 2, 2 * GRID_W, WIN_R * GRID_W)


def _ssm_tables(a_re, a_im, log_dt, b_re, b_im, c_re, c_im, d_skip):
    L = SSM_L
    hp = lax.Precision.HIGHEST
    lam_re, lam_im = a_re.astype(F32), a_im.astype(F32)
    dt = jnp.exp(log_dt.astype(F32))[..., None]
    ldt_re, ldt_im = lam_re * dt, lam_im * dt

    def apow(kk):
        mag = jnp.exp(ldt_re * kk)
        return mag * jnp.cos(ldt_im * kk), mag * jnp.sin(ldt_im * kk)

    pw_re, pw_im = apow(jnp.arange(L + 1, dtype=F32)[:, None, None, None])
    den = lam_re * lam_re + lam_im * lam_im
    co_re = ((pw_re[1] - 1.0) * lam_re + pw_im[1] * lam_im) / den
    co_im = (pw_im[1] * lam_re - (pw_re[1] - 1.0) * lam_im) / den
    bm_re, bm_im = b_re.astype(F32), b_im.astype(F32)
    bb_re = co_re[..., None] * bm_re - co_im[..., None] * bm_im
    bb_im = co_re[..., None] * bm_im + co_im[..., None] * bm_re
    cm_re, cm_im = c_re.astype(F32), c_im.astype(F32)

    e_re = cm_re[None] * pw_re[:L, :, :, None, :] - cm_im[None] * pw_im[:L, :, :, None, :]
    e_im = cm_re[None] * pw_im[:L, :, :, None, :] + cm_im[None] * pw_re[:L, :, :, None, :]
    kern = (jnp.einsum('ldgxp,dgpc->ldgxc', e_re, bb_re, precision=hp)
            - jnp.einsum('ldgxp,dgpc->ldgxc', e_im, bb_im, precision=hp))
    kern = kern.reshape(L, 2, N_SG, 2, HALF_GROUPS, SSM_GROUP, SSM_GROUP)
    s_idx = jnp.arange(L)[:, None]
    t_idx = jnp.arange(L)[None, :]
    lag = jnp.arange(L)
    sel_f = ((t_idx - s_idx)[:, :, None] == lag).astype(F32)
    sel_b = ((s_idx - t_idx)[:, :, None] == lag).astype(F32)
    t_small = (jnp.einsum('stl,lqhgxc->qhsgctx', sel_f, kern[:, 0], precision=hp)
               + jnp.einsum('stl,lqhgxc->qhsgctx', sel_b, kern[:, 1], precision=hp))
    t_small = t_small.reshape(N_SG, 2, HALF_W, LANES)

    def cmul(p_re, p_im, w_re, w_im):
        return p_re * w_re - p_im * w_im, p_re * w_im + p_im * w_re

    dn_re, dn_im = apow(L - jnp.arange(L + 1, dtype=F32)[:, None, None, None])

    def powers(lo, d, descending):
        if descending:
            return dn_re[L + 1 - lo - L:L + 1 - lo, d], dn_im[L + 1 - lo - L:L + 1 - lo, d]
        return pw_re[lo:lo + L, d], pw_im[lo:lo + L, d]

    parts = []
    for d, flip in ((0, True), (1, False)):
        p_re, p_im = powers(0, d, flip)
        parts += list(cmul(p_re[:, :, None, :], p_im[:, :, None, :],
                           bb_re[d].transpose(0, 2, 1)[None], bb_im[d].transpose(0, 2, 1)[None]))
    wb_small = jnp.stack(parts, axis=3)
    wb_small = wb_small.reshape(L, N_SG, 2, HALF_GROUPS, SSM_GROUP, 4 * STATE_P).transpose(1, 2, 0, 3, 4, 5)
    wb_small = wb_small.reshape(N_SG, 2, HALF_W, 4 * STATE_P)

    parts = []
    for d, flip in ((0, False), (1, True)):
        p_re, p_im = powers(1, d, flip)
        z_re, z_im = cmul(p_re[:, :, None, :], p_im[:, :, None, :], cm_re[d][None], cm_im[d][None])
        parts += [z_re, -z_im]
    wc_small = jnp.stack(parts, axis=0).reshape(4, L, N_SG, 2, HALF_GROUPS, SSM_GROUP, STATE_P)
    wc_small = wc_small.transpose(2, 3, 0, 4, 6, 1, 5).reshape(N_SG, 2, 4 * HALF_STATE, LANES)

    lane = jnp.arange(LANES)
    col = jnp.arange(HALF_W)
    rep_lane = ((lane[:, None] // SSM_GROUP == col[None, :] // HALF_LANES)
                & (lane[:, None] % SSM_GROUP == col[None, :] % SSM_GROUP))
    st = jnp.arange(4 * STATE_P)
    scol = jnp.arange(4 * HALF_STATE)
    rep_state = ((st[:, None] // STATE_P == scol[None, :] // HALF_STATE)
                 & (st[:, None] % STATE_P == scol[None, :] % STATE_P))
    lane_group = (col % HALF_LANES) // SSM_GROUP
    state_group = (scol % HALF_STATE) // STATE_P

    def expand(small, rep, row_group, col_group):
        big = jnp.einsum('qhrk,kc->qhrc', small.astype(BF16), rep.astype(BF16), preferred_element_type=F32)
        return jnp.where(row_group[:, None] == col_group[None, :], big, 0.0).astype(BF16)

    t_mat = expand(t_small, rep_lane, lane_group, lane_group)
    wb_mat = expand(wb_small, rep_state, lane_group, state_group)
    wc_mat = expand(wc_small, rep_lane, state_group, lane_group)

    row = jnp.arange(SCAN_ROWS)
    sh = jnp.array([1, 2, 4])
    ones = jnp.ones((1, SCAN_ROWS), bool)
    keep = jnp.stack([jnp.concatenate([row[None, :] >= sh[:, None], ones]),
                      jnp.concatenate([row[None, :] <= SCAN_ROWS - 1 - sh[:, None], ones])])
    shifts = jnp.broadcast_to(sh[:, None], (3, SCAN_ROWS))
    expo = L * jnp.stack([jnp.concatenate([shifts, row[None, :] + 1]),
                          jnp.concatenate([shifts, SCAN_ROWS - row[None, :]])]).astype(F32)
    e = expo[:, :, :, None, None]
    mag = jnp.exp(ldt_re[:, None, None] * e)
    k5 = keep[:, :, :, None, None]
    cst = jnp.stack([jnp.where(k5, mag * jnp.cos(ldt_im[:, None, None] * e), 0.0),
                     jnp.where(k5, mag * jnp.sin(ldt_im[:, None, None] * e), 0.0)], axis=2)
    cst = cst.reshape(2, 4, 2, SCAN_ROWS, N_SG, SG_STATE).transpose(4, 0, 1, 2, 3, 5)

    dsk = jnp.tile(d_skip.astype(F32).reshape(N_SG, 2, 1, HALF_LANES), (1, 1, L, 1)).reshape(N_SG, 1, L * LANES)
    return t_mat, wb_mat, wc_mat, cst, dsk


def _ssm_kernel(u_ref, t_ref, wb_ref, wc_ref, cst_ref, d_ref, y_ref, stf_ref, stb_ref, *, n_chunks, mm_rows):
    n_mm = n_chunks // mm_rows
    tiles = mm_rows // SCAN_ROWS
    half = SG_STATE
    row_id = lax.broadcasted_iota(I32, (SCAN_ROWS, half), 0)
    zero = jnp.zeros((1, half), F32)

    def block(c):
        return pl.ds(pl.multiple_of(c * mm_rows, mm_rows), mm_rows)

    def scan_tile(st_ref, d, row0, carry):
        rows = pl.ds(pl.multiple_of(row0, SCAN_ROWS), SCAN_ROWS)
        hs = HALF_STATE
        xr = jnp.concatenate([st_ref[rows, 0:hs], st_ref[rows, 2 * hs:3 * hs]], axis=1)
        xi = jnp.concatenate([st_ref[rows, hs:2 * hs], st_ref[rows, 3 * hs:4 * hs]], axis=1)
        for si, sh in enumerate((1, 2, 4)):
            ar, ai = cst_ref[d, si, 0], cst_ref[d, si, 1]
            shift = sh if d == 0 else SCAN_ROWS - sh
            pr, pi = pltpu.roll(xr, shift, 0), pltpu.roll(xi, shift, 0)
            xr, xi = xr + (ar * pr - ai * pi), xi + (ar * pi + ai * pr)
        cr, ci = carry
        ar, ai = cst_ref[d, 3, 0], cst_ref[d, 3, 1]
        xr, xi = xr + (ar * cr - ai * ci), xi + (ar * ci + ai * cr)
        if d == 0:
            edge, shift, last = 0, 1, SCAN_ROWS - 1
        else:
            edge, shift, last = SCAN_ROWS - 1, SCAN_ROWS - 1, 0
        sr = jnp.where(row_id == edge, cr, pltpu.roll(xr, shift, 0))
        si_ = jnp.where(row_id == edge, ci, pltpu.roll(xi, shift, 0))
        st_ref[rows, 0:hs], st_ref[rows, 2 * hs:3 * hs] = sr[:, :hs], sr[:, hs:]
        st_ref[rows, hs:2 * hs], st_ref[rows, 3 * hs:4 * hs] = si_[:, :hs], si_[:, hs:]
        return xr[last:last + 1, :], xi[last:last + 1, :]

    halves = [(h, slice(h * HALF_W, (h + 1) * HALF_W)) for h in range(2)]

    def inject(st_ref, d, c, u=None):
        for h, cols in halves:
            uh = u_ref[block(c), cols] if u is None else u[:, cols]
            st_ref[block(c), cols] = _dot(uh, wb_ref[h, :, 2 * d * HALF_STATE:2 * (d + 1) * HALF_STATE])

    def eject(st_ref, d, c):
        for h, cols in halves:
            y_ref[block(c), cols] += _dot(st_ref[block(c), cols].astype(BF16),
                                          wc_ref[h, 2 * d * HALF_STATE:2 * (d + 1) * HALF_STATE, :])

    def inject_fwd(c, carry):
        inject(stf_ref, 0, c)
        return carry

    lax.fori_loop(0, n_mm, inject_fwd, 0)

    def forward(c, carry):
        u = u_ref[block(c), :]
        for h, cols in halves:
            y_ref[block(c), cols] = _dot(u[:, cols], t_ref[h]) + u[:, cols].astype(F32) * d_ref[:, cols]
        inject(stb_ref, 1, c, u)
        for k in range(tiles):
            carry = scan_tile(stf_ref, 0, c * mm_rows + k * SCAN_ROWS, carry)
        return carry

    lax.fori_loop(0, n_mm, forward, (zero, zero))

    def backward(i, carry):
        c = n_mm - 1 - i
        eject(stf_ref, 0, c)
        for k in reversed(range(tiles)):
            carry = scan_tile(stb_ref, 1, c * mm_rows + k * SCAN_ROWS, carry)
        return carry

    lax.fori_loop(0, n_mm, backward, (zero, zero))

    def eject_bwd(c, carry):
        eject(stb_ref, 1, c)
        return carry

    lax.fori_loop(0, n_mm, eject_bwd, 0)


def _ssm(u, tabs):
    t_mat, wb_mat, wc_mat, cst, dsk = tabs
    b, _, n_chunks, lw = u.shape
    mm_rows = min(256, n_chunks)
    assert n_chunks % mm_rows == 0 and n_chunks % SCAN_ROWS == 0
    one = pl.Buffered(1)
    wspec = lambda shape: pl.BlockSpec((None,) + shape, lambda q, i: (q,) + tuple(0 for _ in shape),
                                       pipeline_mode=one)
    return pl.pallas_call(
        functools.partial(_ssm_kernel, n_chunks=n_chunks, mm_rows=mm_rows),
        grid=(N_SG, b),
        in_specs=[pl.BlockSpec((None, None, n_chunks, lw), lambda q, i: (i, q, 0, 0)),
                  wspec((2, HALF_W, HALF_W)), wspec((2, HALF_W, 4 * HALF_STATE)), wspec((2, 4 * HALF_STATE, HALF_W)),
                  wspec((2, 4, 2, SCAN_ROWS, SG_STATE)), wspec((1, lw))],
        out_specs=pl.BlockSpec((None, None, n_chunks, lw), lambda q, i: (i, q, 0, 0)),
        out_shape=jax.ShapeDtypeStruct((b, N_SG, n_chunks, lw), F32),
        scratch_shapes=[pltpu.VMEM((n_chunks, 2 * SG_STATE), F32)] * 2,
        compiler_params=_cparams(("arbitrary", "arbitrary")),
        name="ssm",
    )(u, t_mat, wb_mat, wc_mat, cst, dsk)


def _merge_kernel(x_ref, attn_ref, y_ref, ga_ref, gs_ref, g1_ref, sc_ref, sh_ref, nf_ref,
                  wab_ref, wglu_ref, wout_ref, wr_ref, x1_ref, h2_ref, lg_ref, gel_scr):
    sub = x_ref.shape[0] // MERGE_SPLIT
    crows = sub // SSM_L
    blocks = [(k, slice(k * sub, (k + 1) * sub)) for k in range(MERGE_SPLIT)]

    ab = [_dot(attn_ref[rows, :], wab_ref[...]) for _, rows in blocks]

    low = lax.broadcasted_iota(I32, (crows, LANES), 1) < HALF_LANES
    gel = []
    for k, rows in blocks:
        for sg in range(N_SG):
            for sp in range(SSM_L // 2):
                t0 = y_ref[sg, k * crows:(k + 1) * crows, sp * LANES:(sp + 1) * LANES]
                t1 = y_ref[sg, k * crows:(k + 1) * crows, HALF_W + sp * LANES:HALF_W + (sp + 1) * LANES]
                pair = (jnp.where(low, t0, pltpu.roll(t1, HALF_LANES, 1)),
                        jnp.where(low, pltpu.roll(t0, HALF_LANES, 1), t1))
                for s, y in zip((2 * sp, 2 * sp + 1), pair):
                    gel_scr[sg, k * sub + s * crows:k * sub + (s + 1) * crows, :] = (
                        0.5 * y * (1.0 + jnp.tanh(math.sqrt(2.0 / math.pi) * (y + 0.044715 * (y * y * y)))))
        gel.append(jnp.concatenate(
            [jnp.concatenate([gel_scr[sg, pl.ds(k * sub + j, SSM_L, stride=crows), :] for j in range(crows)], axis=0)
             for sg in range(N_SG)], axis=1).astype(BF16))
    glu = [_dot(g, wglu_ref[...]) for g in gel]
    merged = []
    for (k, rows), ab_k, glu_k in zip(blocks, ab, glu):
        sb = glu_k[:, :D_MODEL] * _sigmoid(glu_k[:, D_MODEL:])
        merged.append((ga_ref[rows, :].astype(F32) * ab_k + gs_ref[rows, :].astype(F32) * sb).astype(BF16))
    mixed = [_dot(m, wout_ref[...]) for m in merged]
    h2s = []
    for (k, rows), mix in zip(blocks, mixed):
        x1 = x_ref[rows, :] + g1_ref[...] * mix
        x1_ref[rows, :] = x1
        ms = jnp.mean(x1 * x1, axis=-1, keepdims=True)
        h2 = x1 * lax.rsqrt(ms + EPS) * nf_ref[...] * (1.0 + sc_ref[...]) + sh_ref[...]
        h2_ref[rows, :D_MODEL] = h2.astype(BF16)
        h2s.append(h2)
    for (k, rows), h2 in zip(blocks, h2s):
        lg = _dot3(h2, wr_ref[...])
        lg_ref[rows, :] = lg
        valid = lax.broadcasted_iota(I32, lg.shape, 1) < N_EXPERTS
        m = jnp.max(jnp.where(valid, lg, -jnp.inf), axis=-1, keepdims=True)
        ex = jnp.where(valid, jnp.exp(lg - m), 0.0)
        aff = ex / jnp.sum(ex, axis=-1, keepdims=True)
        hi = aff.astype(BF16).astype(F32)
        mid = (aff - hi).astype(BF16).astype(F32)
        lo = (aff - hi - mid).astype(BF16).astype(F32)
        parts = hi + pltpu.roll(mid, N_EXPERTS, 1) + pltpu.roll(lo, 2 * N_EXPERTS, 1)
        h2_ref[rows, D_MODEL:] = parts.astype(BF16)


def _merge(x, attn, y4, ga, gs, g1, sc2, sh2, norm_ffn, wab, wglu, wout, wr_pad, bt):
    b, n, d = x.shape
    tok = lambda w: pl.BlockSpec((None, bt, w), lambda i, j: (i, j, 0))
    mod = pl.BlockSpec((None, 1, d), lambda i, j: (i, 0, 0))
    full = lambda shape: pl.BlockSpec(shape, lambda i, j: tuple(0 for _ in shape))
    return pl.pallas_call(
        _merge_kernel,
        grid=(b, n // bt),
        in_specs=[tok(d), tok(D_ATTN),
                  pl.BlockSpec((None, N_SG, bt // SSM_L, SSM_L * LANES), lambda i, j: (i, 0, j, 0)),
                  tok(d), tok(d), mod, mod, mod, full((1, d)),
                  full((D_ATTN, d)), full((D_SSM, 2 * d)), full((d, d)), full((d, LANES))],
        out_specs=[tok(d), tok(d + LANES), tok(LANES)],
        out_shape=[jax.ShapeDtypeStruct((b, n, d), F32), jax.ShapeDtypeStruct((b, n, d + LANES), BF16),
                   jax.ShapeDtypeStruct((b, n, LANES), F32)],
        scratch_shapes=[pltpu.VMEM((N_SG, bt, LANES), F32)],
        compiler_params=_cparams(("parallel", "parallel")),
        name="merge",
    )(x, attn, y4, ga, gs, g1, sc2, sh2, norm_ffn, wab, wglu, wout, wr_pad)


def _route_kernel(lg_ref, tri_ref, slot_ref, cnt_ref, run_ref, *, cap):
    lg = lg_ref[...]
    m = jnp.max(lg, axis=1, keepdims=True)
    e = jnp.exp(lg - m)
    aff = e / jnp.sum(e, axis=1, keepdims=True)

    def count(mask):
        c = jnp.sum(jnp.where(mask, 1.0, 0.0), axis=0, keepdims=True)
        return jnp.sum(c, axis=2, keepdims=True)

    def as_float(bits):
        return lax.bitcast_convert_type(bits, F32)

    def bit_step(i, thr):
        cand = thr | (jnp.int32(1) << (30 - i))
        return jnp.where(count(aff >= as_float(cand)) >= cap, cand, thr)

    thr3 = as_float(lax.fori_loop(0, 31, bit_step, jnp.zeros((1, N_EXPERTS, 1), I32)))
    need = cap - count(aff > thr3)
    n_chunks = lg.shape[0]

    def before(flag):
        f = jnp.where(flag, 1.0, 0.0)
        inc = _dot(f.reshape(n_chunks * N_EXPERTS, CUM_W).astype(BF16), tri_ref[...])
        run_ref[...] = inc.reshape(n_chunks, N_EXPERTS, CUM_W)

        def add_earlier_chunks(c, total):
            x = run_ref[c] + total
            run_ref[c] = x
            return x[:, CUM_W - 1:CUM_W]

        lax.fori_loop(0, n_chunks, add_earlier_chunks, jnp.zeros((N_EXPERTS, 1), F32))
        return run_ref[...] - f

    tie = aff == thr3
    sel = (aff > thr3) | (tie & (before(tie) < need))
    pos = before(sel)
    slot_ref[...] = jnp.where(sel, pos, -1.0).astype(I32)
    cnt_ref[...] = pos.astype(I32)


def _route(logits_c, tri, cap):
    nc = logits_c.shape[0]
    shp = (nc, N_EXPERTS, CUM_W)
    return pl.pallas_call(
        functools.partial(_route_kernel, cap=cap),
        grid=(1,),
        in_specs=[pl.BlockSpec(shp, lambda i: (0, 0, 0)), pl.BlockSpec((CUM_W, CUM_W), lambda i: (0, 0))],
        out_specs=[pl.BlockSpec(shp, lambda i: (0, 0, 0))] * 2,
        out_shape=[jax.ShapeDtypeStruct(shp, I32)] * 2,
        scratch_shapes=[pltpu.VMEM(shp, F32)],
        compiler_params=_cparams(("arbitrary",)),
        name="route",
    )(logits_c, tri)


def _gather_kernel(offs_ref, h_ref, slot_ref, xe_ref, *, nb, cap, win, bt):
    eg = pl.program_id(0)
    step = pl.program_id(1)

    @pl.when(step == 0)
    def _():
        xe_ref[...] = jnp.zeros_like(xe_ref)

    riota = lax.broadcasted_iota(I32, (win, 1), 0)
    blocks_per_step = h_ref.shape[0] // bt

    for sb in range(blocks_per_step):
        blk = step * blocks_per_step + sb
        rows = slice(sb * bt, (sb + 1) * bt)

        def window(ee, w, blk=blk, rows=rows):
            off = offs_ref[(eg * GATHER_EXPERTS + ee) * (nb + 1) + blk]
            nominal = (off // BF16_ROWS) * BF16_ROWS + w * win
            start = pl.multiple_of(jnp.minimum(nominal, cap - win), BF16_ROWS)
            slot = slot_ref[ee, :, rows]
            hit = (slot == riota + start) & (slot >= nominal)
            return jnp.where(hit, 1.0, 0.0).astype(BF16), start

        firsts = [window(ee, 0) for ee in range(GATHER_EXPERTS)]
        picked = _dot(jnp.concatenate([oh for oh, _ in firsts], axis=0), h_ref[rows, :]).astype(BF16)
        for ee, (_, start) in enumerate(firsts):
            xe_ref[ee, pl.ds(start, win), :] += picked[ee * win:(ee + 1) * win]

        for ee in range(GATHER_EXPERTS):
            off = offs_ref[(eg * GATHER_EXPERTS + ee) * (nb + 1) + blk]
            end = offs_ref[(eg * GATHER_EXPERTS + ee) * (nb + 1) + blk + 1]
            n_win = (end - (off // BF16_ROWS) * BF16_ROWS + win - 1) // win

            def extra(w, carry, ee=ee, window=window, rows=rows):
                onehot, start = window(ee, w)
                xe_ref[ee, pl.ds(start, win), :] += _dot(onehot, h_ref[rows, :]).astype(BF16)
                return carry

            lax.fori_loop(1, n_win, extra, 0)


def _gather(offs_flat, h2, slot3, cap, bt):
    t, d = h2.shape
    nb = t // bt
    win = min(GATHER_WIN, cap)
    ge = GATHER_EXPERTS
    step_rows = bt * (GATHER_BLOCKS if nb % GATHER_BLOCKS == 0 else 1)
    gs = pltpu.PrefetchScalarGridSpec(
        num_scalar_prefetch=1,
        grid=(N_EXPERTS // ge, t // step_rows),
        in_specs=[pl.BlockSpec((step_rows, d), lambda e, j, offs: (j, 0)),
                  pl.BlockSpec((ge, 1, step_rows), lambda e, j, offs: (e, 0, j))],
        out_specs=pl.BlockSpec((ge, cap, d), lambda e, j, offs: (e, 0, 0), pipeline_mode=pl.Buffered(1)),
    )
    return pl.pallas_call(
        functools.partial(_gather_kernel, nb=nb, cap=cap, win=win, bt=bt),
        grid_spec=gs,
        out_shape=jax.ShapeDtypeStruct((N_EXPERTS, cap, d), BF16),
        compiler_params=_cparams(("arbitrary", "arbitrary")),
        name="gather",
    )(offs_flat, h2, slot3)


def _ffn_kernel(x_ref, wg_ref, wu_ref, wd_ref, y_ref, *, fchunk):
    x = x_ref[:, :D_MODEL]
    acc = jnp.zeros((x.shape[0], D_MODEL), F32)
    for f in range(D_EXPERT // fchunk):
        fs = slice(f * fchunk, (f + 1) * fchunk)
        a = _dot(x, wg_ref[:, fs])
        u = _dot(x, wu_ref[:, fs])
        hmid = (a * _sigmoid(a) * u).astype(BF16)
        acc = acc + _dot(hmid, wd_ref[fs, :])
    parts = x_ref[:, D_MODEL:].astype(F32)
    lane = lax.broadcasted_iota(I32, parts.shape, 1)
    mine = (lane % N_EXPERTS == pl.program_id(0)) & (lane < 3 * N_EXPERTS)
    gate = jnp.sum(jnp.where(mine, parts, 0.0), axis=-1, keepdims=True)
    y_ref[...] = (gate * acc).astype(BF16)


def _ffn(xe, wg, wu, wd, tm):
    e, cap, dx = xe.shape
    d = D_MODEL
    return pl.pallas_call(
        functools.partial(_ffn_kernel, fchunk=512),
        grid=(e, cap // tm),
        in_specs=[pl.BlockSpec((None, tm, dx), lambda i, j: (i, j, 0)),
                  pl.BlockSpec((None, d, D_EXPERT), lambda i, j: (i, 0, 0)),
                  pl.BlockSpec((None, d, D_EXPERT), lambda i, j: (i, 0, 0)),
                  pl.BlockSpec((None, D_EXPERT, d), lambda i, j: (i, 0, 0))],
        out_specs=pl.BlockSpec((None, tm, d), lambda i, j: (i, j, 0)),
        out_shape=jax.ShapeDtypeStruct((e, cap, d), BF16),
        compiler_params=_cparams(("parallel", "parallel")),
        name="ffn",
    )(xe, wg, wu, wd)


def _window_copy(ye_hbm, buf, sem, e, start, win):
    return pltpu.make_async_copy(ye_hbm.at[e, pl.ds(start, win), :], buf, sem)


def _combine_kernel(offs_ref, x1_ref, slot_ref, g2_ref, ye_hbm, o_ref, ybuf, xbuf, lhs, sems, xsem,
                    *, nb, cap, win):
    blk = pl.program_id(0)

    def first_window(b_, e):
        off = offs_ref[e * (nb + 1) + b_]
        a0 = (off // BF16_ROWS) * BF16_ROWS
        return a0, pl.multiple_of(jnp.minimum(a0, cap - win), BF16_ROWS)

    def window_copies(b_, half):
        return [_window_copy(ye_hbm, ybuf.at[half, pl.ds(e * win, win)], sems.at[half, e], e,
                             first_window(b_, e)[1], win) for e in range(N_EXPERTS)]

    @pl.when(blk == 0)
    def _():
        for cp in window_copies(0, 0):
            cp.start()

    @pl.when(blk + 1 < nb)
    def _():
        for cp in window_copies(blk + 1, (blk + 1) % 2):
            cp.start()

    starts = [first_window(blk, e) for e in range(N_EXPERTS)]
    liota = lax.broadcasted_iota(I32, (1, win), 1)
    slots = slot_ref[...]
    for e in range(N_EXPERTS):
        hit = slots[:, e:e + 1] == liota + starts[e][1]
        lhs[:, e * win:(e + 1) * win] = jnp.where(hit, 1.0, 0.0).astype(BF16)
    for cp in window_copies(blk, blk % 2):
        cp.wait()
    o_ref[...] = x1_ref[...] + g2_ref[...] * _dot(lhs[...], ybuf[blk % 2])

    for e in range(N_EXPERTS):
        a0 = starts[e][0]
        end = offs_ref[e * (nb + 1) + blk + 1]
        n_win = (end - a0 + win - 1) // win

        def extra(w, carry, e=e, a0=a0):
            nominal = a0 + w * win
            st = pl.multiple_of(jnp.minimum(nominal, cap - win), BF16_ROWS)
            cp = _window_copy(ye_hbm, xbuf, xsem, e, st, win)
            cp.start()
            cp.wait()
            scol = slot_ref[:, e:e + 1]
            hit = (scol == liota + st) & (scol >= nominal)
            o_ref[...] += g2_ref[...] * _dot(jnp.where(hit, 1.0, 0.0).astype(BF16), xbuf[...])
            return carry

        lax.fori_loop(1, n_win, extra, 0)


def _combine(offs_flat, x1, slot_t, g2, ye, n_per_batch, bt):
    t, d = x1.shape
    nb = t // bt
    cap = ye.shape[1]
    win = min(SLOT_WIN, cap)
    per = n_per_batch // bt
    gs = pltpu.PrefetchScalarGridSpec(
        num_scalar_prefetch=1,
        grid=(nb,),
        in_specs=[pl.BlockSpec((bt, d), lambda j, offs: (j, 0)),
                  pl.BlockSpec((bt, N_EXPERTS), lambda j, offs: (j, 0)),
                  pl.BlockSpec((None, 1, d), lambda j, offs: (j // per, 0, 0)),
                  pl.BlockSpec(memory_space=pl.ANY)],
        out_specs=pl.BlockSpec((bt, d), lambda j, offs: (j, 0)),
        scratch_shapes=[pltpu.VMEM((2, N_EXPERTS * win, d), BF16), pltpu.VMEM((win, d), BF16),
                        pltpu.VMEM((bt, N_EXPERTS * win), BF16),
                        pltpu.SemaphoreType.DMA((2, N_EXPERTS)), pltpu.SemaphoreType.DMA(())],
    )
    return pl.pallas_call(
        functools.partial(_combine_kernel, nb=nb, cap=cap, win=win),
        grid_spec=gs,
        out_shape=jax.ShapeDtypeStruct((t, d), F32),
        compiler_params=_cparams(("arbitrary",)),
        name="combine",
    )(offs_flat, x1, slot_t, g2, ye)


def _prep_weights(w_in, q_norm, k_norm, rpb, ssm_params, w_glu, w_attn_br, w_out, w_router,
                  w_exp_gate, w_exp_up, w_exp_down):
    head = jnp.arange(D_ATTN) // HEAD_DIM
    return dict(
        w_in=w_in.astype(BF16),
        qg=jnp.tile(q_norm.astype(F32), N_HEADS).reshape(1, D_ATTN),
        kg=jnp.tile(k_norm.astype(F32), N_HEADS).reshape(1, D_ATTN),
        ones_bd=(head[:, None] == head[None, :]).astype(BF16),
        bias_tab=_attn_bias_table(rpb),
        ssm=_ssm_tables(*ssm_params),
        wglu=w_glu.astype(BF16), wab=w_attn_br.astype(BF16), wout=w_out.astype(BF16),
        wr=jnp.pad(w_router.astype(F32), ((0, 0), (0, LANES - N_EXPERTS))),
        wg=w_exp_gate.astype(BF16), wu=w_exp_up.astype(BF16), wd=w_exp_down.astype(BF16),
        tri=(jnp.arange(CUM_W)[:, None] <= jnp.arange(CUM_W)[None, :]).astype(BF16),
    )


def _token_block(n, want):
    bt = min(want, n)
    assert n % bt == 0
    return bt


def _encoder_layer(x, c, w_ada, b_ada, norm_mix, norm_ffn, wts):
    b, n, d = x.shape
    t = b * n
    cap = EC_CAPACITY * t // N_EXPERTS

    c_pad = jnp.pad(c.astype(F32), ((0, (-b) % SUBLANES), (0, 0)))
    mod = _ada(c_pad, w_ada, b_ada)[:b]
    sh1, sc1, g1, sh2, sc2, g2 = [m.reshape(b, 1, d) for m in jnp.split(mod, 6, axis=-1)]

    bt = _token_block(n, 512)
    q, k, v, u4, ga, gs = _inproj(x, sc1, sh1, norm_mix.reshape(1, d), wts["w_in"], wts["qg"], wts["kg"],
                                  wts["ones_bd"], bt)
    attn = _attention(q, k, v, wts["bias_tab"])
    y4 = _ssm(u4, wts["ssm"])
    x1, h2, logits = _merge(x, attn, y4, ga, gs, g1, sc2, sh2, norm_ffn.reshape(1, d),
                            wts["wab"], wts["wglu"], wts["wout"], wts["wr"], bt)

    lg = logits.reshape(t, LANES)[:, :N_EXPERTS]
    lg_c = lg.reshape(t // CUM_W, CUM_W, N_EXPERTS).transpose(0, 2, 1)
    slot_c, cnt_c = _route(lg_c, wts["tri"], cap)
    slot_et = slot_c.transpose(1, 0, 2).reshape(N_EXPERTS, t)
    slot_te = slot_et.T

    bt2 = _token_block(t, 512)
    cnt_at_block = cnt_c.transpose(1, 0, 2).reshape(N_EXPERTS, t)[:, ::bt2]
    offs = jnp.concatenate([cnt_at_block, jnp.full((N_EXPERTS, 1), cap, I32)], axis=-1)
    offs_flat = offs.reshape(-1).astype(I32)

    xe = _gather(offs_flat, h2.reshape(t, d + LANES), slot_et.reshape(N_EXPERTS, 1, t), cap, bt2)
    ye = _ffn(xe, wts["wg"], wts["wu"], wts["wd"], _token_block(cap, 1024))
    out = _combine(offs_flat, x1.reshape(t, d), slot_te, g2, ye, n, bt2)
    return out.reshape(b, n, d)


def kernel(x_prompt, x_sample, c_prompt, c_sample, w_ada, b_ada, norm_mix, norm_ffn, w_in, q_norm, k_norm, rpb,
           ssm_a_re, ssm_a_im, ssm_log_dt, ssm_b_re, ssm_b_im, ssm_c_re, ssm_c_im, ssm_d, w_glu, w_attn_br,
           w_out, w_router, w_exp_gate, w_exp_up, w_exp_down):
    y_prompt, y_sample = x_prompt, x_sample
    for layer in range(w_ada.shape[0]):
        ssm_params = tuple(p[layer] for p in (ssm_a_re, ssm_a_im, ssm_log_dt, ssm_b_re, ssm_b_im,
                                              ssm_c_re, ssm_c_im, ssm_d))
        wts = _prep_weights(w_in[layer], q_norm[layer], k_norm[layer], rpb[layer], ssm_params, w_glu[layer],
                            w_attn_br[layer], w_out[layer], w_router[layer], w_exp_gate[layer],
                            w_exp_up[layer], w_exp_down[layer])
        y_prompt = _encoder_layer(y_prompt, c_prompt, w_ada[layer], b_ada[layer], norm_mix[layer],
                                  norm_ffn[layer], wts)
        y_sample = _encoder_layer(y_sample, c_sample, w_ada[layer], b_ada[layer], norm_mix[layer],
                                  norm_ffn[layer], wts)
    return (y_prompt, y_sample)
```
